```python
import jax, jax.numpy as jnp
from jax import lax
import numpy as np

D_MODEL = 1024
BATCH = 32
SEQ = 256
DEPTH = 1
DEC_BATCH = 8
DEC_SEQ = 4096
PAST_LEN = 512

GRID_W = 64
CHUNK = 128
ROWS_PER_CHUNK = CHUNK // GRID_W
D_MIX = D_MODEL
D_A = D_MIX // 2
D_B = D_MIX - D_A
HEAD_A = 64
H_A = D_A // HEAD_A
H_B = 8
HEAD_B = D_B // H_B
LORA_W = 64
LORA_A = 64
D_IN = 3 * D_A + D_A + 2 * LORA_W + 2 * LORA_A + 3 * D_B
NORM_EPS = 1e-6
GN_EPS = 6.4e-4
DECAY_OFFSET = 0.5
L2_EPS = 1e-12

kernel_name = 'bidir_rwkv7_chunk_gmlp_hybrid_step'


def _rmsnorm(x, g):
    x32 = x.astype(jnp.float32)
    y = x32 * lax.rsqrt(jnp.mean(x32 * x32, axis=-1, keepdims=True) + NORM_EPS)
    return (y * g.astype(jnp.float32)).astype(x.dtype)


def _layernorm(x, g, b):
    x32 = x.astype(jnp.float32)
    mu = jnp.mean(x32, axis=-1, keepdims=True)
    var = jnp.mean(jnp.square(x32 - mu), axis=-1, keepdims=True)
    y = (x32 - mu) * lax.rsqrt(var + NORM_EPS)
    return (y * g.astype(jnp.float32) + b.astype(jnp.float32)).astype(x.dtype)


def _modulation(cvec, w_mod, b_mod):
    m = jax.nn.silu(cvec) @ w_mod + b_mod
    return jnp.split(m, 3, axis=-1)


def _split_proj(p):
    sizes = (3 * D_A, D_A, LORA_W, LORA_W, LORA_A, LORA_A, D_B, D_B, D_B)
    idx, acc = [], 0
    for s in sizes[:-1]:
        acc += s
        idx.append(acc)
    return jnp.split(p, idx, axis=-1)


def _token_shift(x, mu):
    x_prev = jnp.pad(x[:, :-1], ((0, 0), (1, 0), (0, 0)))
    x_next = jnp.pad(x[:, 1:], ((0, 0), (0, 1), (0, 0)))
    return x + mu[0] * (x_prev - x) + mu[1] * (x_next - x)


def _delta_scan(r, k, v, decay, kk, kka, s0, reverse):
    def step(s, inp):
        r_t, k_t, v_t, w_t, kk_t, b_t = inp
        sa = jnp.einsum('bhvk,bhk->bhv', s, -kk_t)
        s = (s * w_t[:, :, None, :] + sa[..., None] * b_t[:, :, None, :]
             + v_t[..., None] * k_t[:, :, None, :])
        return s, jnp.einsum('bhvk,bhk->bhv', s, r_t)
    xs = tuple(jnp.swapaxes(t, 0, 1) for t in (r, k, v, decay, kk, kka))
    s_final, ys = lax.scan(step, s0, xs, reverse=reverse)
    return jnp.swapaxes(ys, 0, 1), s_final


def _rwkv7_branch(rkv, wd, ad, s0, p):
    B, L, _ = rkv.shape
    f32 = jnp.float32
    heads = lambda t: t.reshape(B, L, H_A, HEAD_A).astype(f32)
    hv = lambda t: t.reshape(H_A, HEAD_A).astype(f32)
    r, k, v = (heads(t) for t in jnp.split(rkv, 3, axis=-1))
    y = jnp.zeros_like(r)
    bonus = jnp.zeros_like(r)
    finals = []
    for d in range(2):
        w_log = -jax.nn.softplus(-(p['w0'][d] + jnp.tanh(wd[d]) @ p['w_up'][d])) - DECAY_OFFSET
        decay = jnp.exp(-jnp.exp(heads(w_log)))
        a = heads(jax.nn.sigmoid(p['a0'][d] + ad[d] @ p['a_up'][d]))
        kk = k * hv(p['k_k'][d])
        kk = kk * lax.rsqrt(jnp.sum(kk * kk, axis=-1, keepdims=True) + L2_EPS)
        k_d = k * (1.0 + (a - 1.0) * hv(p['k_a'][d]))
        y_d, s_d = _delta_scan(r, k_d, v, decay, kk, kk * a, s0[d], reverse=(d == 1))
        y = y + y_d
        bonus = bonus + jnp.sum(r * k_d * p['r_k'][d].astype(f32), axis=-1, keepdims=True) * v
        finals.append(s_d)
    mu = jnp.mean(y, axis=-1, keepdims=True)
    var = jnp.mean(jnp.square(y - mu), axis=-1, keepdims=True)
    y = (y - mu) * lax.rsqrt(var + GN_EPS) * hv(p['gn_w']) + hv(p['gn_b']) + bonus
    return y.reshape(B, L, D_A).astype(rkv.dtype), finals[0], finals[1]


def _chunk_gmlp(u, vb, p):
    B, L, _ = u.shape
    rows = L // GRID_W
    n_chunks = rows // ROWS_PER_CHUNK
    vn = _layernorm(vb, p['sgu_ln_g'], p['sgu_ln_b']).reshape(B, n_chunks, CHUNK, H_B, HEAD_B)
    s = jnp.einsum('gpq,bnqgc->bnpgc', p['w_s'], vn) + p['b_s'].T[None, None, :, :, None]
    return u * s.reshape(B, L, D_B)


def _mixer_layer(x, mod, s0, p):
    shift, scale, gate = mod
    h = _rmsnorm(x, p['ln_pre']) * (1.0 + scale) + shift
    proj = h @ p['w_in']
    rkv, z_a, wd_f, wd_b, ad_f, ad_b, u, vb, z_b = _split_proj(proj)
    rkv = _token_shift(rkv, p['ts_mu'])
    y_a, s_f, s_b = _rwkv7_branch(rkv, (wd_f, wd_b), (ad_f, ad_b), s0, p)
    y_b = _chunk_gmlp(u, vb, p)
    mixed = jnp.concatenate([y_a * jax.nn.silu(z_a), y_b * jax.nn.silu(z_b)], axis=-1)
    out = mixed @ p['w_out']
    return x + _rmsnorm(out, p['ln_post']) * gate, s_f, s_b


def setup_inputs(seed: int = 0) -> dict:
    key = jax.random.key(seed)
    ks = jax.random.split(key, 32)
    f32 = jnp.float32
    nrm = lambda k, shape, s: s * jax.random.normal(k, shape, f32)
    st_shape = (DEC_BATCH, DEPTH, H_A, HEAD_A, HEAD_A)
    return {
        'x_prompt': nrm(ks[0], (BATCH, SEQ, D_MODEL), 1.0),
        'x_sample': nrm(ks[1], (DEC_BATCH, DEC_SEQ, D_MODEL), 1.0),
        'c': nrm(ks[2], (DEC_BATCH, D_MODEL), 1.0),
        'state_fwd': nrm(ks[3], st_shape, 1.0),
        'state_bwd': nrm(ks[4], st_shape, 1.0),
        'c_ctx': nrm(ks[5], (D_MODEL,), 1.0),
        'ln_pre': 1.0 + nrm(ks[6], (DEPTH, D_MODEL), 0.05),
        'ln_post': 1.0 + nrm(ks[7], (DEPTH, D_MODEL), 0.05),
        'w_mod': nrm(ks[8], (DEPTH, D_MODEL, 3 * D_MODEL), 0.5 * D_MODEL ** -0.5),
        'b_mod': nrm(ks[9], (DEPTH, 3 * D_MODEL), 0.02),
        'w_in': nrm(ks[10], (DEPTH, D_MODEL, D_IN), D_MODEL ** -0.5),
        'ts_mu': 0.25 + nrm(ks[11], (DEPTH, 2, 3 * D_A), 0.1),
        'w0': nrm(ks[12], (DEPTH, 2, D_A), 0.5),
        'w_up': nrm(ks[13], (DEPTH, 2, LORA_W, D_A), 0.5 * LORA_W ** -0.5),
        'a0': nrm(ks[14], (DEPTH, 2, D_A), 0.2),
        'a_up': nrm(ks[15], (DEPTH, 2, LORA_A, D_A), 0.5 * LORA_A ** -0.5),
        'k_k': 0.85 + nrm(ks[16], (DEPTH, 2, D_A), 0.05),
        'k_a': 1.0 + nrm(ks[17], (DEPTH, 2, D_A), 0.05),
        'r_k': nrm(ks[18], (DEPTH, 2, H_A, HEAD_A), 0.1),
        'gn_w': 1.0 + nrm(ks[19], (DEPTH, D_A), 0.05),
        'gn_b': nrm(ks[20], (DEPTH, D_A), 0.02),
        'sgu_ln_g': 1.0 + nrm(ks[21], (DEPTH, D_B), 0.05),
        'sgu_ln_b': nrm(ks[22], (DEPTH, D_B), 0.02),
        'w_s': nrm(ks[23], (DEPTH, H_B, CHUNK, CHUNK), CHUNK ** -0.5),
        'b_s': 1.0 + nrm(ks[24], (DEPTH, H_B, CHUNK), 0.1),
        'w_out': nrm(ks[25], (DEPTH, D_MIX, D_MODEL), D_MIX ** -0.5),
    }


def reference(x_prompt, x_sample, c, state_fwd, state_bwd, c_ctx,
              ln_pre, ln_post, w_mod, b_mod, w_in, ts_mu, w0, w_up, a0, a_up,
              k_k, k_a, r_k, gn_w, gn_b, sgu_ln_g, sgu_ln_b, w_s, b_s, w_out):
    f32 = jnp.float32
    y_prompt, y_sample = x_prompt, x_sample
    zero_state = jnp.zeros((x_prompt.shape[0], H_A, HEAD_A, HEAD_A), f32)
    new_f, new_b = [], []
    for l in range(DEPTH):
        p = {'ln_pre': ln_pre[l], 'ln_post': ln_post[l], 'w_in': w_in[l], 'ts_mu': ts_mu[l],
             'w0': w0[l], 'w_up': w_up[l], 'a0': a0[l], 'a_up': a_up[l], 'k_k': k_k[l],
             'k_a': k_a[l], 'r_k': r_k[l], 'gn_w': gn_w[l], 'gn_b': gn_b[l],
             'sgu_ln_g': sgu_ln_g[l], 'sgu_ln_b': sgu_ln_b[l], 'w_s': w_s[l], 'b_s': b_s[l],
             'w_out': w_out[l]}
        mod_ctx = tuple(m[None, None, :] for m in _modulation(c_ctx, w_mod[l], b_mod[l]))
        mod_lat = tuple(m[:, None, :] for m in _modulation(c, w_mod[l], b_mod[l]))
        y_prompt, s_f, s_b = _mixer_layer(y_prompt, mod_ctx, (zero_state, zero_state), p)
        new_f.append(s_f)
        new_b.append(s_b)
        s0_lat = (state_fwd[:, l].astype(f32), state_bwd[:, l].astype(f32))
        y_sample, _, _ = _mixer_layer(y_sample, mod_lat, s0_lat, p)
    new_state_fwd = jnp.stack(new_f, axis=1).astype(x_prompt.dtype)
    new_state_bwd = jnp.stack(new_b, axis=1).astype(x_prompt.dtype)
    return (y_prompt, y_sample, new_state_fwd, new_state_bwd)
```

```python
import functools
import math

import jax
import jax.numpy as jnp
from jax import lax
from jax.experimental import pallas as pl
from jax.experimental.pallas import tpu as pltpu

F32 = jnp.float32
BF16 = jnp.bfloat16

D_MODEL = 1024
D_A = 512
D_B = 512
HEAD = 64
H_A = D_A // HEAD
H_B = 8
LORA = 64
GMLP_CHUNK = 128
D_RKV = 3 * D_A
D_LORA = 4 * LORA
D_UVZ = 3 * D_B
D_IN = D_RKV + D_A + D_LORA + D_UVZ
NORM_EPS = 1e-6
GN_EPS = 6.4e-4
L2_EPS = 1e-12
DECAY_SCALE = math.exp(-0.5)

SCAN_CHUNK = 64
GROUP_LANES = 256
HEADS_PER_GROUP = GROUP_LANES // HEAD
N_GROUPS = D_A // GROUP_LANES
HALO = 8

VMEM_LIMIT = 56 * 1024 * 1024


def _dot(a, b):
    return jnp.dot(a, b, preferred_element_type=F32)


def _dot_nt(a, b):
    return lax.dot_general(a, b, (((1,), (1,)), ((), ())), preferred_element_type=F32)


def _dot_tn(a, b):
    return lax.dot_general(a, b, (((0,), (0,)), ((), ())), preferred_element_type=F32)


def _split2(x):
    hi = x.astype(BF16)
    lo = (x - hi.astype(F32)).astype(BF16)
    return hi, lo


def _split3(x):
    hi = x.astype(BF16)
    r1 = x - hi.astype(F32)
    mid = r1.astype(BF16)
    lo = (r1 - mid.astype(F32)).astype(BF16)
    return hi, mid, lo


def _sigmoid(x):
    return 1.0 / (1.0 + jnp.exp(-x))


def _mod_kernel(c_ref, w_ref, b_ref, o_ref):
    c = c_ref[...]
    s = c * _sigmoid(c)
    sh, sl = _split2(s)
    wh, wl = _split2(w_ref[...])
    o_ref[...] = _dot(sh, wh) + _dot(sh, wl) + _dot(sl, wh) + b_ref[...]


def _modulation(cvecs, w_mod, b_mod):
    n = cvecs.shape[0]
    nblk = 3
    return pl.pallas_call(
        _mod_kernel,
        grid=(nblk,),
        in_specs=[
            pl.BlockSpec((n, D_MODEL), lambda j: (0, 0)),
            pl.BlockSpec((D_MODEL, D_MODEL), lambda j: (0, j)),
            pl.BlockSpec((1, D_MODEL), lambda j: (0, j)),
        ],
        out_specs=pl.BlockSpec((n, D_MODEL), lambda j: (0, j)),
        out_shape=jax.ShapeDtypeStruct((n, 3 * D_MODEL), F32),
        compiler_params=pltpu.CompilerParams(vmem_limit_bytes=VMEM_LIMIT),
        name="modulation",
    )(cvecs, w_mod, b_mod)


def _inproj_kernel(x_ref, mod_ref, lnpre_ref, w_ref, rkv_ref, za_ref, lora_ref, uvz_ref):
    x = x_ref[0]
    shift = mod_ref[0, 0:1, :]
    scale = mod_ref[0, 1:2, :]
    gain = lnpre_ref[...] * (1.0 + scale)
    ms = jnp.mean(x * x, axis=-1, keepdims=True)
    h = (x * lax.rsqrt(ms + NORM_EPS)) * gain + shift
    hb = h.astype(BF16)
    o = 0
    rkv_ref[0] = _dot(hb, w_ref[:, o:o + D_RKV]); o += D_RKV
    za_ref[0] = _dot(hb, w_ref[:, o:o + D_A]); o += D_A
    lora_ref[0] = _dot(hb, w_ref[:, o:o + D_LORA]); o += D_LORA
    uvz_ref[0] = _dot(hb, w_ref[:, o:o + D_UVZ])


def _inproj(x, mod, ln_pre, w_in_bf, tm):
    nb, L, _ = x.shape
    tok = lambda d: pl.BlockSpec((1, tm, d), lambda b, i: (b, i, 0))
    shp = lambda d: jax.ShapeDtypeStruct((nb, L, d), F32)
    return pl.pallas_call(
        _inproj_kernel,
        grid=(nb, L // tm),
        in_specs=[
            tok(D_MODEL),
            pl.BlockSpec((1, 3, D_MODEL), lambda b, i: (b, 0, 0)),
            pl.BlockSpec((1, D_MODEL), lambda b, i: (0, 0)),
            pl.BlockSpec((D_MODEL, D_IN), lambda b, i: (0, 0)),
        ],
        out_specs=[tok(D_RKV), tok(D_A), tok(D_LORA), tok(D_UVZ)],
        out_shape=[shp(D_RKV), shp(D_A), shp(D_LORA), shp(D_UVZ)],
        compiler_params=pltpu.CompilerParams(
            dimension_semantics=("arbitrary", "arbitrary"), vmem_limit_bytes=VMEM_LIMIT),
        name="inproj",
    )(x, mod, ln_pre, w_in_bf)


def _block_diag(x, bd):
    return jnp.concatenate([x] * HEADS_PER_GROUP, axis=0) * bd


def _head_sum(x, ones_bd):
    outs = []
    for g in range(N_GROUPS):
        hi, lo = _split2(x[:, g * GROUP_LANES:(g + 1) * GROUP_LANES])
        outs.append(_dot(hi, ones_bd) + _dot(lo, ones_bd))
    return jnp.concatenate(outs, axis=1)


def _scan_kernel(*refs, reverse, finalize, zero_init, emit_state, nbat, tb):
    it = iter(refs)
    rkv_ref, hprev_ref, hnext_ref, lora_ref = next(it), next(it), next(it), next(it)
    s0_ref = None if zero_init else next(it)
    mu_ref, pv_ref, wup_ref, aup_ref = next(it), next(it), next(it), next(it)
    tri_ref, ones_ref, bd_ref = next(it), next(it), next(it)
    if finalize:
        yf_ref, za_ref, pv2_ref, aupf_ref = next(it), next(it), next(it), next(it)
    y_ref = next(it)
    sout_ref = next(it) if emit_state else None
    s_scr, xs_scr = next(it), next(it)

    j = pl.program_id(1)
    nj = pl.num_programs(1)
    jj = (nj - 1 - j) if reverse else j
    n_chunks = tb // SCAN_CHUNK
    C = SCAN_CHUNK

    @pl.when(j == 0)
    def _():
        if zero_init:
            s_scr[...] = jnp.zeros_like(s_scr)
        else:
            s_scr[...] = s0_ref[...]

    xs_scr[:, 0:HALO, :] = jnp.where(jj == 0, 0.0, hprev_ref[...])
    xs_scr[:, HALO:HALO + tb, :] = rkv_ref[...]
    xs_scr[:, HALO + tb:, :] = jnp.where(jj == nj - 1, 0.0, hnext_ref[...])

    row = lax.broadcasted_iota(jnp.int32, (C, GROUP_LANES), 0)
    col = lax.broadcasted_iota(jnp.int32, (C, GROUP_LANES), 1)
    scol = col % HEAD
    hcol = col // HEAD
    if reverse:
        m_strict, m_incl = scol > row, scol >= row
    else:
        m_strict, m_incl = scol < row, scol <= row
    eye = jnp.where(scol == row, 1.0, 0.0).astype(F32)
    last_row = 0 if reverse else C - 1

    mu0 = mu_ref[0:1, :]
    mu1 = mu_ref[1:2, :]
    w0, a0, k_k, k_a = pv_ref[0:1, :], pv_ref[1:2, :], pv_ref[2:3, :], pv_ref[3:4, :]

    def chunk_body(ci, carry):
        c = (n_chunks - 1 - ci) if reverse else ci
        base = pl.multiple_of(c * C, C)
        tri = tri_ref[...]
        ones_bd = ones_ref[...]
        bd = bd_ref[...]
        for bi in range(nbat):
            ext = xs_scr[bi, pl.ds(base, C + 2 * HALO), :]
            cur = ext[HALO:HALO + C]
            prv = pltpu.roll(ext, 1, 0)[HALO:HALO + C]
            nxt = pltpu.roll(ext, C + 2 * HALO - 1, 0)[HALO:HALO + C]
            sh = cur + mu0 * (prv - cur) + mu1 * (nxt - cur)
            r, k, v = sh[:, 0:D_A], sh[:, D_A:2 * D_A], sh[:, 2 * D_A:3 * D_A]

            lo = lora_ref[bi, pl.ds(base, C), :]
            wd = jnp.tanh(lo[:, 0:2 * LORA]).astype(BF16)
            ad = lo[:, 2 * LORA:4 * LORA].astype(BF16)
            ld = -DECAY_SCALE * _sigmoid(w0 + _dot(wd, wup_ref[...]))
            a = _sigmoid(a0 + _dot(ad, aup_ref[...]))

            kk = k * k_k
            kk = kk * lax.rsqrt(_head_sum(kk * kk, ones_bd) + L2_EPS)
            k_d = k * (1.0 + (a - 1.0) * k_a)
            b = kk * a

            h3 = _split3(ld)
            cum = _dot(tri, h3[0]) + _dot(tri, h3[1]) + _dot(tri, h3[2])
            tot = cum[last_row:last_row + 1, :]
            e_in = jnp.exp(cum)
            e_ex = jnp.exp(cum - ld)
            e_neg = jnp.exp(-cum)
            e_rem = jnp.exp(tot - cum)
            g_tot = jnp.exp(tot)

            at = (-kk * e_ex).astype(BF16)
            rt = (r * e_in).astype(BF16)
            bt = (b * e_neg).astype(BF16)
            kt = (k_d * e_neg).astype(BF16)
            bg = (b * e_rem).astype(BF16)
            kg = (k_d * e_rem).astype(BF16)
            vb = v.astype(BF16)

            y_groups = []
            for g in range(N_GROUPS):
                sl = slice(g * GROUP_LANES, (g + 1) * GROUP_LANES)
                ar = jnp.concatenate([at[:, sl], rt[:, sl]], axis=0)
                a_b = _dot_nt(ar, _block_diag(bt[:, sl], bd))
                a_k = _dot_nt(ar, _block_diag(kt[:, sl], bd))
                a_ab = jnp.where(m_strict, a_b[0:C], 0.0)
                a_rb = jnp.where(m_incl, a_b[C:2 * C], 0.0)
                a_ak = jnp.where(m_strict, a_k[0:C], 0.0)
                a_rk = jnp.where(m_incl, a_k[C:2 * C], 0.0)

                t = eye + a_ab
                pb = a_ab.astype(BF16)
                p = _dot(pb, _block_diag(pb, bd))
                for _ in range(int(math.log2(C)) - 2):
                    pb = p.astype(BF16)
                    pt = _dot(jnp.concatenate([pb, t.astype(BF16)], axis=0), _block_diag(pb, bd))
                    p = pt[0:C]
                    t = t + pt[C:2 * C]
                t = t + _dot(t.astype(BF16), _block_diag(p.astype(BF16), bd))

                s_c = s_scr[bi, :, sl]
                xs_ = _dot_nt(ar, _block_diag(s_c.astype(BF16), bd))
                vbd = _block_diag(vb[:, sl], bd)
                w = xs_[0:C] + _dot(a_ak.astype(BF16), vbd)
                u = _dot(t.astype(BF16), _block_diag(w.astype(BF16), bd))
                ub = u.astype(BF16)
                y = (xs_[C:2 * C] + _dot(a_rb.astype(BF16), _block_diag(ub, bd))
                     + _dot(a_rk.astype(BF16), vbd))
                y_groups.append(y)

                full = _dot_tn(jnp.concatenate([ub, vb[:, sl]], axis=0),
                               jnp.concatenate([bg[:, sl], kg[:, sl]], axis=0))
                upd = jnp.zeros((HEAD, GROUP_LANES), F32)
                for h in range(HEADS_PER_GROUP):
                    upd = jnp.where(hcol == h, full[h * HEAD:(h + 1) * HEAD, :], upd)
                s_scr[bi, :, sl] = s_c * g_tot[:, sl] + upd

            y = jnp.concatenate(y_groups, axis=1)
            if finalize:
                a0f, k_af = pv2_ref[0:1, :], pv2_ref[1:2, :]
                r_kf, r_kb = pv2_ref[2:3, :], pv2_ref[3:4, :]
                gn_w, gn_b = pv2_ref[4:5, :], pv2_ref[5:6, :]
                a_f = _sigmoid(a0f + _dot(ad, aupf_ref[...]))
                k_df = k * (1.0 + (a_f - 1.0) * k_af)
                bonus = _head_sum(r * (k_df * r_kf + k_d * r_kb), ones_bd) * v
                yt = yf_ref[bi, pl.ds(base, C), :] + y
                mean = _head_sum(yt, ones_bd) * (1.0 / HEAD)
                yc = yt - mean
                var = _head_sum(yc * yc, ones_bd) * (1.0 / HEAD)
                yn = yc * lax.rsqrt(var + GN_EPS) * gn_w + gn_b + bonus
                z = za_ref[bi, pl.ds(base, C), :]
                y = yn * (z * _sigmoid(z))
            y_ref[bi, pl.ds(base, C), :] = y
        return carry

    lax.fori_loop(0, n_chunks, chunk_body, 0)

    if emit_state:
        @pl.when(j == nj - 1)
        def _():
            sout_ref[...] = s_scr[...]


def _scan(rkv, lora, s0c, consts, *, reverse, finalize, emit_state, nbat, tb, yf=None, za=None):
    nb, L, _ = rkv.shape
    nj = L // tb
    hb = tb // HALO
    nh = L // HALO
    zero_init = s0c is None
    d = 1 if reverse else 0
    if reverse:
        jmap = lambda j: nj - 1 - j
    else:
        jmap = lambda j: j
    tok = lambda w: pl.BlockSpec((nbat, tb, w), lambda b, j: (b, jmap(j), 0))
    full2 = lambda a: pl.BlockSpec(a.shape, lambda b, j: (0, 0))
    st_spec = pl.BlockSpec((nbat, HEAD, D_A), lambda b, j: (b, 0, 0))

    args = [rkv, rkv, rkv, lora]
    in_specs = [
        tok(D_RKV),
        pl.BlockSpec((nbat, HALO, D_RKV), lambda b, j: (b, jnp.maximum(jmap(j) * hb - 1, 0), 0)),
        pl.BlockSpec((nbat, HALO, D_RKV), lambda b, j: (b, jnp.minimum((jmap(j) + 1) * hb, nh - 1), 0)),
        tok(D_LORA),
    ]
    if not zero_init:
        args.append(s0c)
        in_specs.append(st_spec)
    small = [consts["mu"], consts["pv"][d], consts["wup"][d], consts["aup"][d],
             consts["tri"][d], consts["ones_bd"], consts["bd"]]
    args += small
    in_specs += [full2(a) for a in small]
    if finalize:
        args += [yf, za, consts["pv2"], consts["aup"][0]]
        in_specs += [tok(D_A), tok(D_A), full2(consts["pv2"]), full2(consts["aup"][0])]

    out_shape = [jax.ShapeDtypeStruct((nb, L, D_A), F32)]
    out_specs = [tok(D_A)]
    if emit_state:
        out_shape.append(jax.ShapeDtypeStruct((nb, HEAD, D_A), F32))
        out_specs.append(st_spec)

    kern = functools.partial(_scan_kernel, reverse=reverse, finalize=finalize, zero_init=zero_init,
                             emit_state=emit_state, nbat=nbat, tb=tb)
    outs = pl.pallas_call(
        kern,
        grid=(nb // nbat, nj),
        in_specs=in_specs,
        out_specs=out_specs,
        out_shape=out_shape,
        scratch_shapes=[pltpu.VMEM((nbat, HEAD, D_A), F32),
                        pltpu.VMEM((nbat, tb + 2 * HALO, D_RKV), F32)],
        compiler_params=pltpu.CompilerParams(
            dimension_semantics=("arbitrary", "arbitrary"), vmem_limit_bytes=VMEM_LIMIT),
        name="scan_bwd" if reverse else "scan_fwd",
    )(*args)
    return outs if emit_state else (outs[0], None)


def _out_kernel(x_ref, uvz_ref, ya_ref, mod_ref, lnpost_ref, sgu_ref, ws_ref, bs_ref, wout_ref, o_ref, *, tm):
    gate = mod_ref[0, 2:3, :]
    ln_g = sgu_ref[0:1, :]
    ln_b = sgu_ref[1:2, :]
    lane = lax.broadcasted_iota(jnp.int32, (GMLP_CHUNK, 128), 1)
    low = lane < (D_B // H_B)
    for q in range(tm // GMLP_CHUNK):
        rows = slice(q * GMLP_CHUNK, (q + 1) * GMLP_CHUNK)
        u = uvz_ref[0, rows, 0:D_B]
        vb = uvz_ref[0, rows, D_B:2 * D_B]
        zb = uvz_ref[0, rows, 2 * D_B:3 * D_B]
        mu = jnp.mean(vb, axis=-1, keepdims=True)
        vc = vb - mu
        var = jnp.mean(vc * vc, axis=-1, keepdims=True)
        vn = (vc * lax.rsqrt(var + NORM_EPS)) * ln_g + ln_b
        parts = []
        for pr in range(H_B // 2):
            vp = vn[:, pr * 128:(pr + 1) * 128]
            rhs = jnp.concatenate([jnp.where(low, vp, 0.0), jnp.where(low, 0.0, vp)], axis=0)
            parts.append(_dot(ws_ref[pr], rhs.astype(BF16)))
        s = jnp.concatenate(parts, axis=1) + bs_ref[...]
        yb = u * s * (zb * _sigmoid(zb))
        mixed = jnp.concatenate([ya_ref[0, rows, :], yb], axis=1).astype(BF16)
        out = _dot(mixed, wout_ref[...])
        ms = jnp.mean(out * out, axis=-1, keepdims=True)
        o_ref[0, rows, :] = x_ref[0, rows, :] + (out * lax.rsqrt(ms + NORM_EPS)) * lnpost_ref[...] * gate


def _out_stage(x, uvz, ya, mod, ln_post, sgu, ws_cat, bs_x, w_out_bf, tm):
    nb, L, _ = x.shape
    tok = lambda d: pl.BlockSpec((1, tm, d), lambda b, i: (b, i, 0))
    full = lambda a: pl.BlockSpec(a.shape, lambda b, i: (0,) * a.ndim)
    return pl.pallas_call(
        functools.partial(_out_kernel, tm=tm),
        grid=(nb, L // tm),
        in_specs=[tok(D_MODEL), tok(D_UVZ), tok(D_A),
                  pl.BlockSpec((1, 3, D_MODEL), lambda b, i: (b, 0, 0)),
                  full(ln_post), full(sgu), full(ws_cat), full(bs_x), full(w_out_bf)],
        out_specs=tok(D_MODEL),
        out_shape=jax.ShapeDtypeStruct((nb, L, D_MODEL), F32),
        compiler_params=pltpu.CompilerParams(
            dimension_semantics=("arbitrary", "arbitrary"), vmem_limit_bytes=VMEM_LIMIT),
        name="out_stage",
    )(x, uvz, ya, mod, ln_post, sgu, ws_cat, bs_x, w_out_bf)


def _layer(x, mod, s0f, s0b, lw, consts, *, emit_state, nbat, tb, tm_in, tm_out):
    rkv, za, lora, uvz = _inproj(x, mod, lw["ln_pre"], lw["w_in_bf"], tm_in)
    yf, sf = _scan(rkv, lora, s0f, consts, reverse=False, finalize=False,
                   emit_state=emit_state, nbat=nbat, tb=tb)
    ya, sb = _scan(rkv, lora, s0b, consts, reverse=True, finalize=True,
                   emit_state=emit_state, nbat=nbat, tb=tb, yf=yf, za=za)
    y = _out_stage(x, uvz, ya, mod, lw["ln_post"], lw["sgu"], lw["ws_cat"], lw["bs_x"],
                   lw["w_out_bf"], tm_out)
    return y, sf, sb


def _pad_rows(a, n):
    return jnp.concatenate([a, jnp.zeros((n - a.shape[0],) + a.shape[1:], a.dtype)], axis=0)


def _to_compact(s):
    b = s.shape[0]
    return jnp.transpose(s, (0, 2, 1, 3)).reshape(b, HEAD, D_A)


def _from_compact(sc):
    b = sc.shape[0]
    return jnp.transpose(sc.reshape(b, HEAD, H_A, HEAD), (0, 2, 1, 3))


def kernel(x_prompt, x_sample, c, state_fwd, state_bwd, c_ctx, ln_pre, ln_post, w_mod, b_mod, w_in, ts_mu, w0, w_up, a0, a_up, k_k, k_a, r_k, gn_w, gn_b, sgu_ln_g, sgu_ln_b, w_s, b_s, w_out):
    depth = w_in.shape[0]
    batch, seq, _ = x_prompt.shape
    dec_batch, dec_seq, _ = x_sample.shape

    ti = jnp.arange(SCAN_CHUNK)
    tri_f = (ti[None, :] <= ti[:, None]).astype(BF16)
    tri_b = (ti[None, :] >= ti[:, None]).astype(BF16)
    gi = jnp.arange(GROUP_LANES) // HEAD
    ones_bd = (gi[:, None] == gi[None, :]).astype(BF16)
    bd = ones_bd

    cvecs = _pad_rows(jnp.concatenate([c_ctx[None, :], c], axis=0), 16)

    y_ctx = x_prompt
    y_lat = x_sample
    new_f, new_b = [], []
    for l in range(depth):
        mods = _modulation(cvecs, w_mod[l], b_mod[l][None, :]).reshape(16, 3, D_MODEL)
        mod_ctx = mods[0:1]
        mod_lat = mods[1:1 + dec_batch]

        zpad = jnp.zeros((LORA, D_A), F32)
        wup = [jnp.concatenate([w_up[l, 0], zpad], 0).astype(BF16),
               jnp.concatenate([zpad, w_up[l, 1]], 0).astype(BF16)]
        aup = [jnp.concatenate([a_up[l, 0], zpad], 0).astype(BF16),
               jnp.concatenate([zpad, a_up[l, 1]], 0).astype(BF16)]
        pv = [_pad_rows(jnp.stack([w0[l, d], a0[l, d], k_k[l, d], k_a[l, d]]), 8) for d in range(2)]
        pv2 = _pad_rows(jnp.stack([a0[l, 0], k_a[l, 0], r_k[l, 0].reshape(D_A), r_k[l, 1].reshape(D_A),
                                   gn_w[l], gn_b[l]]), 8)
        consts = {"mu": ts_mu[l], "pv": pv, "pv2": pv2, "wup": wup, "aup": aup,
                  "tri": [tri_f, tri_b], "ones_bd": ones_bd, "bd": bd}
        lw = {
            "ln_pre": ln_pre[l][None, :], "ln_post": ln_post[l][None, :],
            "w_in_bf": w_in[l].astype(BF16), "w_out_bf": w_out[l].astype(BF16),
            "sgu": jnp.stack([sgu_ln_g[l], sgu_ln_b[l]]),
            "ws_cat": jnp.concatenate([w_s[l, 0::2], w_s[l, 1::2]], axis=2).astype(BF16),
            "bs_x": jnp.repeat(b_s[l].T, D_B // H_B, axis=1),
        }

        y_ctx, sf, sb = _layer(y_ctx, jnp.broadcast_to(mod_ctx, (batch, 3, D_MODEL)), None, None, lw,
                               consts, emit_state=True, nbat=2, tb=seq, tm_in=seq, tm_out=seq)
        new_f.append(_from_compact(sf))
        new_b.append(_from_compact(sb))

        y_lat, _, _ = _layer(y_lat, mod_lat, _to_compact(state_fwd[:, l]), _to_compact(state_bwd[:, l]),
                             lw, consts, emit_state=False, nbat=2, tb=256, tm_in=512, tm_out=512)

    y_prompt = y_ctx
    new_state_fwd = jnp.stack(new_f, axis=1).astype(x_prompt.dtype)
    new_state_bwd = jnp.stack(new_b, axis=1).astype(x_prompt.dtype)
    return (y_prompt, y_lat, new_state_fwd, new_state_bwd)
```

```python
import functools
import math

import jax
import jax.numpy as jnp
from jax import lax
from jax.experimental import pallas as pl
from jax.experimental.pallas import tpu as pltpu

F32 = jnp.float32
BF16 = jnp.bfloat16

D_MODEL = 1024
D_A = 512
D_B = 512
HEAD = 64
H_A = D_A // HEAD
H_B = 8
LORA = 64
GMLP_CHUNK = 128
D_RKV = 3 * D_A
D_LORA = 4 * LORA
D_UVZ = 3 * D_B
D_IN = D_RKV + D_A + D_LORA + D_UVZ
NORM_EPS = 1e-6
GN_EPS = 6.4e-4
L2_EPS = 1e-12
DECAY_SCALE = math.exp(-0.5)

SCAN_CHUNK = 64
GROUP_LANES = 256
HEADS_PER_GROUP = GROUP_LANES // HEAD
N_GROUPS = D_A // GROUP_LANES
HALO = 8
SUBLANES = 8

VMEM_LIMIT = 56 * 1024 * 1024


def _dot(a, b):
    return jnp.dot(a, b, preferred_element_type=F32)


def _dot_nt(a, b):
    return lax.dot_general(a, b, (((1,), (1,)), ((), ())), preferred_element_type=F32)


def _dot_tn(a, b):
    return lax.dot_general(a, b, (((0,), (0,)), ((), ())), preferred_element_type=F32)


def _split2(x):
    hi = x.astype(BF16)
    lo = (x - hi.astype(F32)).astype(BF16)
    return hi, lo


def _split3(x):
    hi = x.astype(BF16)
    r1 = x - hi.astype(F32)
    mid = r1.astype(BF16)
    lo = (r1 - mid.astype(F32)).astype(BF16)
    return hi, mid, lo


def _sigmoid(x):
    return 1.0 / (1.0 + jnp.exp(-x))


def _mod_kernel(c_ref, w_ref, b_ref, o_ref):
    c = c_ref[...]
    s = c * _sigmoid(c)
    sh, sl = _split2(s)
    wh, wl = _split2(w_ref[...])
    o_ref[...] = _dot(sh, wh) + _dot(sh, wl) + _dot(sl, wh) + b_ref[...]


def _modulation(cvecs, w_mod, b_mod):
    n = cvecs.shape[0]
    nblk = 3
    return pl.pallas_call(
        _mod_kernel,
        grid=(nblk,),
        in_specs=[
            pl.BlockSpec((n, D_MODEL), lambda j: (0, 0)),
            pl.BlockSpec((D_MODEL, D_MODEL), lambda j: (0, j)),
            pl.BlockSpec((1, D_MODEL), lambda j: (0, j)),
        ],
        out_specs=pl.BlockSpec((n, D_MODEL), lambda j: (0, j)),
        out_shape=jax.ShapeDtypeStruct((n, 3 * D_MODEL), F32),
        compiler_params=pltpu.CompilerParams(vmem_limit_bytes=VMEM_LIMIT),
        name="modulation",
    )(cvecs, w_mod, b_mod)


def _inproj_kernel(x_ref, mod_ref, lnpre_ref, w_ref, rkv_ref, za_ref, lora_ref, uvz_ref):
    x = x_ref[0]
    shift = mod_ref[0, 0:1, :]
    scale = mod_ref[0, 1:2, :]
    gain = lnpre_ref[...] * (1.0 + scale)
    ms = jnp.mean(x * x, axis=-1, keepdims=True)
    h = (x * lax.rsqrt(ms + NORM_EPS)) * gain + shift
    hb = h.astype(BF16)
    o = 0
    rkv_ref[0] = _dot(hb, w_ref[:, o:o + D_RKV]); o += D_RKV
    za_ref[0] = _dot(hb, w_ref[:, o:o + D_A]); o += D_A
    lora_ref[0] = _dot(hb, w_ref[:, o:o + D_LORA]); o += D_LORA
    uvz_ref[0] = _dot(hb, w_ref[:, o:o + D_UVZ])


def _inproj(x, mod, ln_pre, w_in_bf, tm):
    nb, L, _ = x.shape
    tok = lambda d: pl.BlockSpec((1, tm, d), lambda b, i: (b, i, 0))
    shp = lambda d: jax.ShapeDtypeStruct((nb, L, d), F32)
    return pl.pallas_call(
        _inproj_kernel,
        grid=(nb, L // tm),
        in_specs=[
            tok(D_MODEL),
            pl.BlockSpec((1, 3, D_MODEL), lambda b, i: (b, 0, 0)),
            pl.BlockSpec((1, D_MODEL), lambda b, i: (0, 0)),
            pl.BlockSpec((D_MODEL, D_IN), lambda b, i: (0, 0)),
        ],
        out_specs=[tok(D_RKV), tok(D_A), tok(D_LORA), tok(D_UVZ)],
        out_shape=[shp(D_RKV), shp(D_A), shp(D_LORA), shp(D_UVZ)],
        compiler_params=pltpu.CompilerParams(
            dimension_semantics=("arbitrary", "arbitrary"), vmem_limit_bytes=VMEM_LIMIT),
        name="inproj",
    )(x, mod, ln_pre, w_in_bf)


def _block_diag(x, bd):
    return jnp.concatenate([x] * HEADS_PER_GROUP, axis=0) * bd


def _head_sum(x, ones_bd):
    rows = x.shape[0]
    parts = []
    for g in range(N_GROUPS):
        parts.extend(_split2(x[:, g * GROUP_LANES:(g + 1) * GROUP_LANES]))
    s = _dot(jnp.concatenate(parts, axis=0), ones_bd)
    outs = [s[(2 * g) * rows:(2 * g + 1) * rows] + s[(2 * g + 1) * rows:(2 * g + 2) * rows]
            for g in range(N_GROUPS)]
    return jnp.concatenate(outs, axis=1)


def _scan_kernel(*refs, reverse, finalize, zero_init, emit_state, nbat, tb):
    it = iter(refs)
    rkv_ref, hprev_ref, hnext_ref, lora_ref = next(it), next(it), next(it), next(it)
    s0_ref = None if zero_init else next(it)
    mu_ref, pv_ref, wup_ref, aup_ref = next(it), next(it), next(it), next(it)
    tri_ref, ones_ref, bd_ref = next(it), next(it), next(it)
    if finalize:
        yf_ref, za_ref, pv2_ref, aupf_ref = next(it), next(it), next(it), next(it)
    y_ref = next(it)
    sout_ref = next(it) if emit_state else None
    s_scr, xs_scr = next(it), next(it)
    at_s, rt_s, bt_s, kt_s, bg_s, kg_s, vb_s = (next(it) for _ in range(7))
    arb_s, ark_s, tt_s, avk_s, gt_s = (next(it) for _ in range(5))
    bon_s = next(it) if finalize else None

    j = pl.program_id(1)
    nj = pl.num_programs(1)
    jj = (nj - 1 - j) if reverse else j
    n_chunks = tb // SCAN_CHUNK
    C = SCAN_CHUNK
    G = GROUP_LANES

    @pl.when(j == 0)
    def _():
        if zero_init:
            s_scr[...] = jnp.zeros_like(s_scr)
        else:
            s_scr[...] = s0_ref[...]

    xs_scr[:, 0:HALO, :] = jnp.where(jj == 0, 0.0, hprev_ref[...])
    xs_scr[:, HALO:HALO + tb, :] = rkv_ref[...]
    xs_scr[:, HALO + tb:, :] = jnp.where(jj == nj - 1, 0.0, hnext_ref[...])

    row = lax.broadcasted_iota(jnp.int32, (C, G), 0)
    col = lax.broadcasted_iota(jnp.int32, (C, G), 1)
    scol = col % HEAD
    hcol = col // HEAD
    if reverse:
        m_strict, m_incl = scol > row, scol >= row
    else:
        m_strict, m_incl = scol < row, scol <= row
    eye = jnp.where(scol == row, 1.0, 0.0).astype(F32)
    last_row = 0 if reverse else C - 1

    mu0 = mu_ref[0:1, :]
    mu1 = mu_ref[1:2, :]
    w0, a0, k_k, k_a = pv_ref[0:1, :], pv_ref[1:2, :], pv_ref[2:3, :], pv_ref[3:4, :]
    ones_bd = ones_ref[...]

    for bi in range(nbat):
        cur = xs_scr[bi, HALO:HALO + tb, :]
        prv = xs_scr[bi, HALO - 1:HALO - 1 + tb, :]
        nxt = xs_scr[bi, HALO + 1:HALO + 1 + tb, :]
        sh = cur + mu0 * (prv - cur) + mu1 * (nxt - cur)
        r, k, v = sh[:, 0:D_A], sh[:, D_A:2 * D_A], sh[:, 2 * D_A:3 * D_A]

        lo = lora_ref[bi]
        wd = jnp.tanh(lo[:, 0:2 * LORA]).astype(BF16)
        ad = lo[:, 2 * LORA:4 * LORA].astype(BF16)
        ld = -DECAY_SCALE * _sigmoid(w0 + _dot(wd, wup_ref[...]))
        a = _sigmoid(a0 + _dot(ad, aup_ref[...]))

        kk = k * k_k
        kk = kk * lax.rsqrt(_head_sum(kk * kk, ones_bd) + L2_EPS)
        k_d = k * (1.0 + (a - 1.0) * k_a)
        b = kk * a

        h3 = _split3(ld)
        tri = tri_ref[...]
        cum = _dot(tri, h3[0]) + _dot(tri, h3[1]) + _dot(tri, h3[2])
        rems = []
        for c in range(n_chunks):
            tot = cum[c * C + last_row:c * C + last_row + 1, :]
            rems.append(jnp.exp(tot - cum[c * C:(c + 1) * C, :]))
            gt_s[bi, c * SUBLANES:(c + 1) * SUBLANES, :] = jnp.broadcast_to(jnp.exp(tot), (SUBLANES, D_A))
        e_rem = jnp.concatenate(rems, axis=0)
        e_neg = jnp.exp(-cum)

        at_s[bi] = (-kk * jnp.exp(cum - ld)).astype(BF16)
        rt_s[bi] = (r * jnp.exp(cum)).astype(BF16)
        bt_s[bi] = (b * e_neg).astype(BF16)
        kt_s[bi] = (k_d * e_neg).astype(BF16)
        bg_s[bi] = (b * e_rem).astype(BF16)
        kg_s[bi] = (k_d * e_rem).astype(BF16)
        vb_s[bi] = v.astype(BF16)

        if finalize:
            a0f, k_af = pv2_ref[0:1, :], pv2_ref[1:2, :]
            r_kf, r_kb = pv2_ref[2:3, :], pv2_ref[3:4, :]
            a_f = _sigmoid(a0f + _dot(ad, aupf_ref[...]))
            k_df = k * (1.0 + (a_f - 1.0) * k_af)
            bon_s[bi] = _head_sum(r * (k_df * r_kf + k_d * r_kb), ones_bd) * v

    chains = [(c, g) for c in range(n_chunks) for g in range(N_GROUPS)]

    def wave(bi, carry):
        bd = bd_ref[...]

        def ld_(ref, c, g):
            return ref[bi, c * C:(c + 1) * C, g * G:(g + 1) * G]

        ar = [jnp.concatenate([ld_(at_s, c, g), ld_(rt_s, c, g)], axis=0) for c, g in chains]
        a_b = [_dot_nt(ar[i], _block_diag(ld_(bt_s, c, g), bd)) for i, (c, g) in enumerate(chains)]
        a_k = [_dot_nt(ar[i], _block_diag(ld_(kt_s, c, g), bd)) for i, (c, g) in enumerate(chains)]
        a_ab = [jnp.where(m_strict, x[0:C], 0.0) for x in a_b]
        a_ak = [jnp.where(m_strict, x[0:C], 0.0).astype(BF16) for x in a_k]
        for i, (c, g) in enumerate(chains):
            arb_s[bi, c * C:(c + 1) * C, g * G:(g + 1) * G] = jnp.where(m_incl, a_b[i][C:2 * C], 0.0).astype(BF16)
            ark_s[bi, c * C:(c + 1) * C, g * G:(g + 1) * G] = jnp.where(m_incl, a_k[i][C:2 * C], 0.0).astype(BF16)

        pb = [x.astype(BF16) for x in a_ab]
        p = [_dot(x, _block_diag(x, bd)) for x in pb]
        for i, (c, g) in enumerate(chains):
            avk_s[bi, c * C:(c + 1) * C, g * G:(g + 1) * G] = _dot(a_ak[i], _block_diag(ld_(vb_s, c, g), bd))
        t = [eye + x for x in a_ab]
        for _ in range(int(math.log2(C)) - 2):
            pb = [x.astype(BF16) for x in p]
            pt = [_dot(jnp.concatenate([pb[i], t[i].astype(BF16)], axis=0), _block_diag(pb[i], bd))
                  for i in range(len(chains))]
            p = [x[0:C] for x in pt]
            t = [t[i] + pt[i][C:2 * C] for i in range(len(chains))]
        for i, (c, g) in enumerate(chains):
            tf = t[i] + _dot(t[i].astype(BF16), _block_diag(p[i].astype(BF16), bd))
            tt_s[bi, c * C:(c + 1) * C, g * G:(g + 1) * G] = tf.astype(BF16)
        return carry

    lax.fori_loop(0, nbat, wave, 0)

    seq_chains = [(bi, g) for bi in range(nbat) for g in range(N_GROUPS)]

    def chunk_step(ci, carry):
        c = (n_chunks - 1 - ci) if reverse else ci
        base = pl.multiple_of(c * C, C)
        gbase = pl.multiple_of(c * SUBLANES, SUBLANES)
        rows = pl.ds(base, C)
        bd = bd_ref[...]
        sls = [slice(g * G, (g + 1) * G) for _, g in seq_chains]
        n = len(seq_chains)

        s_c = [s_scr[bi, :, sls[i]] for i, (bi, g) in enumerate(seq_chains)]
        ar = [jnp.concatenate([at_s[bi, rows, sls[i]], rt_s[bi, rows, sls[i]]], axis=0)
              for i, (bi, g) in enumerate(seq_chains)]
        xs_ = [_dot_nt(ar[i], _block_diag(s_c[i].astype(BF16), bd)) for i in range(n)]
        w = [xs_[i][0:C] + avk_s[bi, rows, sls[i]] for i, (bi, g) in enumerate(seq_chains)]
        u = [_dot(tt_s[bi, rows, sls[i]], _block_diag(w[i].astype(BF16), bd))
             for i, (bi, g) in enumerate(seq_chains)]
        ub = [x.astype(BF16) for x in u]
        vbc = [vb_s[bi, rows, sls[i]] for i, (bi, g) in enumerate(seq_chains)]
        full = [_dot_tn(jnp.concatenate([ub[i], vbc[i]], axis=0),
                        jnp.concatenate([bg_s[bi, rows, sls[i]], kg_s[bi, rows, sls[i]]], axis=0))
                for i, (bi, g) in enumerate(seq_chains)]
        for i, (bi, g) in enumerate(seq_chains):
            upd = jnp.zeros((HEAD, G), F32)
            for h in range(HEADS_PER_GROUP):
                upd = jnp.where(hcol == h, full[i][h * HEAD:(h + 1) * HEAD, :], upd)
            g_tot = gt_s[bi, pl.ds(gbase, SUBLANES), sls[i]][0:1, :]
            s_scr[bi, :, sls[i]] = s_c[i] * g_tot + upd
        for i, (bi, g) in enumerate(seq_chains):
            lhs = jnp.concatenate([arb_s[bi, rows, sls[i]], ark_s[bi, rows, sls[i]]], axis=1)
            rhs = jnp.concatenate([_block_diag(ub[i], bd), _block_diag(vbc[i], bd)], axis=0)
            y_ref[bi, rows, sls[i]] = xs_[i][C:2 * C] + _dot(lhs, rhs)
        return carry

    lax.fori_loop(0, n_chunks, chunk_step, 0)

    if finalize:
        gn_w, gn_b = pv2_ref[4:5, :], pv2_ref[5:6, :]
        for bi in range(nbat):
            yt = yf_ref[bi] + y_ref[bi]
            mean = _head_sum(yt, ones_bd) * (1.0 / HEAD)
            yc = yt - mean
            var = _head_sum(yc * yc, ones_bd) * (1.0 / HEAD)
            yn = yc * lax.rsqrt(var + GN_EPS) * gn_w + gn_b + bon_s[bi]
            z = za_ref[bi]
            y_ref[bi] = yn * (z * _sigmoid(z))

    if emit_state:
        @pl.when(j == nj - 1)
        def _():
            sout_ref[...] = s_scr[...]


def _scan(rkv, lora, s0c, consts, *, reverse, finalize, emit_state, nbat, tb, yf=None, za=None):
    nb, L, _ = rkv.shape
    nj = L // tb
    hb = tb // HALO
    nh = L // HALO
    zero_init = s0c is None
    d = 1 if reverse else 0
    if reverse:
        jmap = lambda j: nj - 1 - j
    else:
        jmap = lambda j: j
    tok = lambda w: pl.BlockSpec((nbat, tb, w), lambda b, j: (b, jmap(j), 0))
    full2 = lambda a: pl.BlockSpec(a.shape, lambda b, j: (0, 0))
    st_spec = pl.BlockSpec((nbat, HEAD, D_A), lambda b, j: (b, 0, 0))

    args = [rkv, rkv, rkv, lora]
    in_specs = [
        tok(D_RKV),
        pl.BlockSpec((nbat, HALO, D_RKV), lambda b, j: (b, jnp.maximum(jmap(j) * hb - 1, 0), 0)),
        pl.BlockSpec((nbat, HALO, D_RKV), lambda b, j: (b, jnp.minimum((jmap(j) + 1) * hb, nh - 1), 0)),
        tok(D_LORA),
    ]
    if not zero_init:
        args.append(s0c)
        in_specs.append(st_spec)
    small = [consts["mu"], consts["pv"][d], consts["wup"][d], consts["aup"][d],
             consts["tri"][(d, tb)], consts["ones_bd"], consts["bd"]]
    args += small
    in_specs += [full2(a) for a in small]
    if finalize:
        args += [yf, za, consts["pv2"], consts["aup"][0]]
        in_specs += [tok(D_A), tok(D_A), full2(consts["pv2"]), full2(consts["aup"][0])]

    out_shape = [jax.ShapeDtypeStruct((nb, L, D_A), F32)]
    out_specs = [tok(D_A)]
    if emit_state:
        out_shape.append(jax.ShapeDtypeStruct((nb, HEAD, D_A), F32))
        out_specs.append(st_spec)

    tokbuf = lambda dt: pltpu.VMEM((nbat, tb, D_A), dt)
    scratch = [pltpu.VMEM((nbat, HEAD, D_A), F32),
               pltpu.VMEM((nbat, tb + 2 * HALO, D_RKV), F32)]
    scratch += [tokbuf(BF16)] * 7
    scratch += [tokbuf(BF16)] * 3 + [tokbuf(F32)]
    scratch += [pltpu.VMEM((nbat, (tb // SCAN_CHUNK) * SUBLANES, D_A), F32)]
    if finalize:
        scratch += [tokbuf(F32)]

    kern = functools.partial(_scan_kernel, reverse=reverse, finalize=finalize, zero_init=zero_init,
                             emit_state=emit_state, nbat=nbat, tb=tb)
    outs = pl.pallas_call(
        kern,
        grid=(nb // nbat, nj),
        in_specs=in_specs,
        out_specs=out_specs,
        out_shape=out_shape,
        scratch_shapes=scratch,
        compiler_params=pltpu.CompilerParams(
            dimension_semantics=("arbitrary", "arbitrary"), vmem_limit_bytes=VMEM_LIMIT),
        name="scan_bwd" if reverse else "scan_fwd",
    )(*args)
    return outs if emit_state else (outs[0], None)


def _out_kernel(x_ref, uvz_ref, ya_ref, mod_ref, lnpost_ref, sgu_ref, ws_ref, bs_ref, wout_ref, o_ref, *, tm):
    gate = mod_ref[0, 2:3, :]
    ln_g = sgu_ref[0:1, :]
    ln_b = sgu_ref[1:2, :]
    lane = lax.broadcasted_iota(jnp.int32, (GMLP_CHUNK, 128), 1)
    low = lane < (D_B // H_B)
    for q in range(tm // GMLP_CHUNK):
        rows = slice(q * GMLP_CHUNK, (q + 1) * GMLP_CHUNK)
        u = uvz_ref[0, rows, 0:D_B]
        vb = uvz_ref[0, rows, D_B:2 * D_B]
        zb = uvz_ref[0, rows, 2 * D_B:3 * D_B]
        mu = jnp.mean(vb, axis=-1, keepdims=True)
        vc = vb - mu
        var = jnp.mean(vc * vc, axis=-1, keepdims=True)
        vn = (vc * lax.rsqrt(var + NORM_EPS)) * ln_g + ln_b
        parts = []
        for pr in range(H_B // 2):
            vp = vn[:, pr * 128:(pr + 1) * 128]
            rhs = jnp.concatenate([jnp.where(low, vp, 0.0), jnp.where(low, 0.0, vp)], axis=0)
            parts.append(_dot(ws_ref[pr], rhs.astype(BF16)))
        s = jnp.concatenate(parts, axis=1) + bs_ref[...]
        yb = u * s * (zb * _sigmoid(zb))
        mixed = jnp.concatenate([ya_ref[0, rows, :], yb], axis=1).astype(BF16)
        out = _dot(mixed, wout_ref[...])
        ms = jnp.mean(out * out, axis=-1, keepdims=True)
        o_ref[0, rows, :] = x_ref[0, rows, :] + (out * lax.rsqrt(ms + NORM_EPS)) * lnpost_ref[...] * gate


def _out_stage(x, uvz, ya, mod, ln_post, sgu, ws_cat, bs_x, w_out_bf, tm):
    nb, L, _ = x.shape
    tok = lambda d: pl.BlockSpec((1, tm, d), lambda b, i: (b, i, 0))
    full = lambda a: pl.BlockSpec(a.shape, lambda b, i: (0,) * a.ndim)
    return pl.pallas_call(
        functools.partial(_out_kernel, tm=tm),
        grid=(nb, L // tm),
        in_specs=[tok(D_MODEL), tok(D_UVZ), tok(D_A),
                  pl.BlockSpec((1, 3, D_MODEL), lambda b, i: (b, 0, 0)),
                  full(ln_post), full(sgu), full(ws_cat), full(bs_x), full(w_out_bf)],
        out_specs=tok(D_MODEL),
        out_shape=jax.ShapeDtypeStruct((nb, L, D_MODEL), F32),
        compiler_params=pltpu.CompilerParams(
            dimension_semantics=("arbitrary", "arbitrary"), vmem_limit_bytes=VMEM_LIMIT),
        name="out_stage",
    )(x, uvz, ya, mod, ln_post, sgu, ws_cat, bs_x, w_out_bf)


def _layer(x, mod, s0f, s0b, lw, consts, *, emit_state, nbat, tb, tm_in, tm_out):
    rkv, za, lora, uvz = _inproj(x, mod, lw["ln_pre"], lw["w_in_bf"], tm_in)
    yf, sf = _scan(rkv, lora, s0f, consts, reverse=False, finalize=False,
                   emit_state=emit_state, nbat=nbat, tb=tb)
    ya, sb = _scan(rkv, lora, s0b, consts, reverse=True, finalize=True,
                   emit_state=emit_state, nbat=nbat, tb=tb, yf=yf, za=za)
    y = _out_stage(x, uvz, ya, mod, lw["ln_post"], lw["sgu"], lw["ws_cat"], lw["bs_x"],
                   lw["w_out_bf"], tm_out)
    return y, sf, sb


def _pad_rows(a, n):
    return jnp.concatenate([a, jnp.zeros((n - a.shape[0],) + a.shape[1:], a.dtype)], axis=0)


def _to_compact(s):
    b = s.shape[0]
    return jnp.transpose(s, (0, 2, 1, 3)).reshape(b, HEAD, D_A)


def _from_compact(sc):
    b = sc.shape[0]
    return jnp.transpose(sc.reshape(b, HEAD, H_A, HEAD), (0, 2, 1, 3))


def _chunk_tri(tb, reverse):
    ti = jnp.arange(tb)
    same = (ti[:, None] // SCAN_CHUNK) == (ti[None, :] // SCAN_CHUNK)
    order = (ti[None, :] >= ti[:, None]) if reverse else (ti[None, :] <= ti[:, None])
    return (same & order).astype(BF16)


def kernel(x_prompt, x_sample, c, state_fwd, state_bwd, c_ctx, ln_pre, ln_post, w_mod, b_mod, w_in, ts_mu, w0, w_up, a0, a_up, k_k, k_a, r_k, gn_w, gn_b, sgu_ln_g, sgu_ln_b, w_s, b_s, w_out):
    depth = w_in.shape[0]
    batch, seq, _ = x_prompt.shape
    dec_batch, dec_seq, _ = x_sample.shape
    tb_ctx, tb_lat = seq, 256

    tri = {(d, tb): _chunk_tri(tb, bool(d)) for d in range(2) for tb in {tb_ctx, tb_lat}}
    gi = jnp.arange(GROUP_LANES) // HEAD
    ones_bd = (gi[:, None] == gi[None, :]).astype(BF16)
    bd = ones_bd

    cvecs = _pad_rows(jnp.concatenate([c_ctx[None, :], c], axis=0), 16)

    y_ctx = x_prompt
    y_lat = x_sample
    new_f, new_b = [], []
    for l in range(depth):
        mods = _modulation(cvecs, w_mod[l], b_mod[l][None, :]).reshape(16, 3, D_MODEL)
        mod_ctx = mods[0:1]
        mod_lat = mods[1:1 + dec_batch]

        zpad = jnp.zeros((LORA, D_A), F32)
        wup = [jnp.concatenate([w_up[l, 0], zpad], 0).astype(BF16),
               jnp.concatenate([zpad, w_up[l, 1]], 0).astype(BF16)]
        aup = [jnp.concatenate([a_up[l, 0], zpad], 0).astype(BF16),
               jnp.concatenate([zpad, a_up[l, 1]], 0).astype(BF16)]
        pv = [_pad_rows(jnp.stack([w0[l, d], a0[l, d], k_k[l, d], k_a[l, d]]), 8) for d in range(2)]
        pv2 = _pad_rows(jnp.stack([a0[l, 0], k_a[l, 0], r_k[l, 0].reshape(D_A), r_k[l, 1].reshape(D_A),
                                   gn_w[l], gn_b[l]]), 8)
        consts = {"mu": ts_mu[l], "pv": pv, "pv2": pv2, "wup": wup, "aup": aup,
                  "tri": tri, "ones_bd": ones_bd, "bd": bd}
        lw = {
            "ln_pre": ln_pre[l][None, :], "ln_post": ln_post[l][None, :],
            "w_in_bf": w_in[l].astype(BF16), "w_out_bf": w_out[l].astype(BF16),
            "sgu": jnp.stack([sgu_ln_g[l], sgu_ln_b[l]]),
            "ws_cat": jnp.concatenate([w_s[l, 0::2], w_s[l, 1::2]], axis=2).astype(BF16),
            "bs_x": jnp.repeat(b_s[l].T, D_B // H_B, axis=1),
        }

        y_ctx, sf, sb = _layer(y_ctx, jnp.broadcast_to(mod_ctx, (batch, 3, D_MODEL)), None, None, lw,
                               consts, emit_state=True, nbat=2, tb=tb_ctx, tm_in=seq, tm_out=seq)
        new_f.append(_from_compact(sf))
        new_b.append(_from_compact(sb))

        y_lat, _, _ = _layer(y_lat, mod_lat, _to_compact(state_fwd[:, l]), _to_compact(state_bwd[:, l]),
                             lw, consts, emit_state=False, nbat=2, tb=tb_lat, tm_in=512, tm_out=512)

    y_prompt = y_ctx
    new_state_fwd = jnp.stack(new_f, axis=1).astype(x_prompt.dtype)
    new_state_bwd = jnp.stack(new_b, axis=1).astype(x_prompt.dtype)
    return (y_prompt, y_lat, new_state_fwd, new_state_bwd)
```

```python
import functools
import math

import jax
import jax.numpy as jnp
from jax import lax
from jax.experimental import pallas as pl
from jax.experimental.pallas import tpu as pltpu

F32 = jnp.float32
BF16 = jnp.bfloat16

D_MODEL = 1024
D_A = 512
D_B = 512
HEAD = 64
H_A = D_A // HEAD
H_B = 8
LORA = 64
GMLP_CHUNK = 128
D_RKV = 3 * D_A
D_LORA = 4 * LORA
D_UVZ = 3 * D_B
D_IN = D_RKV + D_A + D_LORA + D_UVZ
NORM_EPS = 1e-6
GN_EPS = 6.4e-4
L2_EPS = 1e-12
DECAY_SCALE = math.exp(-0.5)

SCAN_CHUNK = 64
GROUP_LANES = 256
HEADS_PER_GROUP = GROUP_LANES // HEAD
N_GROUPS = D_A // GROUP_LANES
HALO = 8
SUBLANES = 8
WAVE_CHUNKS = 2

VMEM_LIMIT = 56 * 1024 * 1024


def _dot(a, b):
    return jnp.dot(a, b, preferred_element_type=F32)


def _dot_nt(a, b):
    return lax.dot_general(a, b, (((1,), (1,)), ((), ())), preferred_element_type=F32)


def _dot_tn(a, b):
    return lax.dot_general(a, b, (((0,), (0,)), ((), ())), preferred_element_type=F32)


def _split2(x):
    hi = x.astype(BF16)
    lo = (x - hi.astype(F32)).astype(BF16)
    return hi, lo


def _sigmoid(x):
    return 1.0 / (1.0 + jnp.exp(-x))


def _mod_kernel(c_ref, w_ref, b_ref, o_ref):
    c = c_ref[...]
    s = c * _sigmoid(c)
    sh, sl = _split2(s)
    wh, wl = _split2(w_ref[...])
    o_ref[...] = _dot(sh, wh) + _dot(sh, wl) + _dot(sl, wh) + b_ref[...]


def _modulation(cvecs, w_mod, b_mod):
    n = cvecs.shape[0]
    nblk = 3
    return pl.pallas_call(
        _mod_kernel,
        grid=(nblk,),
        in_specs=[
            pl.BlockSpec((n, D_MODEL), lambda j: (0, 0)),
            pl.BlockSpec((D_MODEL, D_MODEL), lambda j: (0, j)),
            pl.BlockSpec((1, D_MODEL), lambda j: (0, j)),
        ],
        out_specs=pl.BlockSpec((n, D_MODEL), lambda j: (0, j)),
        out_shape=jax.ShapeDtypeStruct((n, 3 * D_MODEL), F32),
        compiler_params=pltpu.CompilerParams(vmem_limit_bytes=VMEM_LIMIT),
        name="modulation",
    )(cvecs, w_mod, b_mod)


def _inproj_kernel(x_ref, mod_ref, lnpre_ref, w_ref, rkv_ref, za_ref, lora_ref, uvz_ref):
    x = x_ref[0]
    shift = mod_ref[0, 0:1, :]
    scale = mod_ref[0, 1:2, :]
    gain = lnpre_ref[...] * (1.0 + scale)
    ms = jnp.mean(x * x, axis=-1, keepdims=True)
    h = (x * lax.rsqrt(ms + NORM_EPS)) * gain + shift
    hb = h.astype(BF16)
    o = 0
    rkv_ref[0] = _dot(hb, w_ref[:, o:o + D_RKV]); o += D_RKV
    za_ref[0] = _dot(hb, w_ref[:, o:o + D_A]); o += D_A
    lora_ref[0] = _dot(hb, w_ref[:, o:o + D_LORA]); o += D_LORA
    uvz_ref[0] = _dot(hb, w_ref[:, o:o + D_UVZ]).astype(uvz_ref.dtype)


def _inproj(x, mod, ln_pre, w_in_bf, tm):
    nb, L, _ = x.shape
    tok = lambda d: pl.BlockSpec((1, tm, d), lambda b, i: (b, i, 0))
    shp = lambda d, dt=F32: jax.ShapeDtypeStruct((nb, L, d), dt)
    return pl.pallas_call(
        _inproj_kernel,
        grid=(nb, L // tm),
        in_specs=[
            tok(D_MODEL),
            pl.BlockSpec((1, 3, D_MODEL), lambda b, i: (b, 0, 0)),
            pl.BlockSpec((1, D_MODEL), lambda b, i: (0, 0)),
            pl.BlockSpec((D_MODEL, D_IN), lambda b, i: (0, 0)),
        ],
        out_specs=[tok(D_RKV), tok(D_A), tok(D_LORA), tok(D_UVZ)],
        out_shape=[shp(D_RKV), shp(D_A), shp(D_LORA), shp(D_UVZ, BF16)],
        compiler_params=pltpu.CompilerParams(
            dimension_semantics=("arbitrary", "arbitrary"), vmem_limit_bytes=VMEM_LIMIT),
        name="inproj",
    )(x, mod, ln_pre, w_in_bf)


def _block_diag(x, bd):
    return jnp.concatenate([x] * HEADS_PER_GROUP, axis=0) * bd


def _head_sum(x, ones_bd):
    rows = x.shape[0]
    parts = []
    for g in range(N_GROUPS):
        parts.extend(_split2(x[:, g * GROUP_LANES:(g + 1) * GROUP_LANES]))
    s = _dot(jnp.concatenate(parts, axis=0), ones_bd)
    outs = [s[(2 * g) * rows:(2 * g + 1) * rows] + s[(2 * g + 1) * rows:(2 * g + 2) * rows]
            for g in range(N_GROUPS)]
    return jnp.concatenate(outs, axis=1)


def _interleave(threads):
    gens = [g for g, _ in threads]
    total = [n for _, n in threads]
    done = [0] * len(gens)
    alive = [True] * len(gens)
    while any(alive):
        k = min((i for i in range(len(gens)) if alive[i]), key=lambda i: done[i] / total[i])
        try:
            next(gens[k])
            done[k] += 1
        except StopIteration:
            alive[k] = False


def _scan_kernel(*refs, reverse, finalize, zero_init, emit_state, nbat, tb, nj, n_items):
    it = iter(refs)
    rkv_ref, hprev_ref, hnext_ref, lora_ref = next(it), next(it), next(it), next(it)
    s0_ref = None if zero_init else next(it)
    mu_ref, pv_ref, wup_ref, aup_ref = next(it), next(it), next(it), next(it)
    tri_ref, ones_ref, bd_ref = next(it), next(it), next(it)
    if finalize:
        yf_ref, za_ref, pv2_ref, aupf_ref = next(it), next(it), next(it), next(it)
    y_ref = next(it)
    sout_ref = next(it) if emit_state else None
    s_scr = next(it)
    at_s, rt_s, bt_s, kt_s, vb_s, gt_s = (next(it) for _ in range(6))
    bon_s = next(it) if finalize else None
    tt_s, arb_s, avk_s, y0_s = (next(it) for _ in range(4))
    yd_ref = next(it) if finalize else y_ref

    C = SCAN_CHUNK
    G = GROUP_LANES
    n_chunks = tb // C

    i = pl.program_id(0)
    w0_slot = lax.rem(i, 3)
    r1_slot = lax.rem(i + 2, 3)
    r2_slot = lax.rem(i + 1, 3)
    w1_slot = lax.rem(i + 1, 2)
    q2_slot = lax.rem(i, 2)
    item0 = jnp.minimum(i, n_items - 1)
    jpos0 = lax.rem(item0, nj)
    jj0 = (nj - 1 - jpos0) if reverse else jpos0
    item2 = jnp.clip(i - 2, 0, n_items - 1)
    jpos2 = lax.rem(item2, nj)
    first2 = jpos2 == 0
    last2 = jnp.logical_and(jpos2 == nj - 1, i >= 2)

    @pl.when(i == 0)
    def _():
        for ref in (at_s, rt_s, bt_s, kt_s, vb_s, gt_s, tt_s, arb_s, avk_s, y0_s) + ((bon_s,) if finalize else ()):
            ref[...] = jnp.zeros_like(ref)
        s_scr[...] = jnp.zeros_like(s_scr)

    row = lax.broadcasted_iota(jnp.int32, (C, G), 0)
    col = lax.broadcasted_iota(jnp.int32, (C, G), 1)
    scol = col % HEAD
    hcol = col // HEAD
    if reverse:
        m_strict, m_incl = scol > row, scol >= row
    else:
        m_strict, m_incl = scol < row, scol <= row
    eye = jnp.where(scol == row, 1.0, 0.0).astype(F32)
    last_row = 0 if reverse else C - 1
    row_w = lax.broadcasted_iota(jnp.int32, (C, D_RKV), 0)

    mu0 = mu_ref[0:1, :]
    mu1 = mu_ref[1:2, :]
    w0, a0, k_k, k_a = pv_ref[0:1, :], pv_ref[1:2, :], pv_ref[2:3, :], pv_ref[3:4, :]

    def prep_stream():
        for bi in range(nbat):
            for c in range(n_chunks):
                rows = slice(c * C, (c + 1) * C)
                cur = rkv_ref[bi, rows, :]
                if c > 0:
                    prv = rkv_ref[bi, c * C - 1:(c + 1) * C - 1, :]
                else:
                    edge = jnp.where(jj0 == 0, 0.0, hprev_ref[bi, HALO - 1:HALO, :])
                    prv = jnp.where(row_w == 0, edge, pltpu.roll(cur, 1, 0))
                if c < n_chunks - 1:
                    nxt = rkv_ref[bi, c * C + 1:(c + 1) * C + 1, :]
                else:
                    edge = jnp.where(jj0 == nj - 1, 0.0, hnext_ref[bi, 0:1, :])
                    nxt = jnp.where(row_w == C - 1, edge, pltpu.roll(cur, C - 1, 0))
                sh = cur + mu0 * (prv - cur) + mu1 * (nxt - cur)
                r, k, v = sh[:, 0:D_A], sh[:, D_A:2 * D_A], sh[:, 2 * D_A:3 * D_A]
                vb_s[w0_slot, bi, rows, :] = v.astype(BF16)
                yield

                lo = lora_ref[bi, rows, :]
                wd = jnp.tanh(lo[:, 0:2 * LORA]).astype(BF16)
                ad = lo[:, 2 * LORA:4 * LORA].astype(BF16)
                ld = -DECAY_SCALE * _sigmoid(w0 + _dot(wd, wup_ref[...]))
                a = _sigmoid(a0 + _dot(ad, aup_ref[...]))
                kk = k * k_k
                kk2 = _head_sum(kk * kk, ones_ref[...])
                hi, lo2 = _split2(ld)
                cum = _dot(tri_ref[...], jnp.concatenate([hi, lo2], axis=1))
                yield

                cum = cum[:, 0:D_A] + cum[:, D_A:2 * D_A]
                kk = kk * lax.rsqrt(kk2 + L2_EPS)
                k_d = k * (1.0 + (a - 1.0) * k_a)
                tot = cum[last_row:last_row + 1, :]
                gt_s[w0_slot, bi, c * SUBLANES:(c + 1) * SUBLANES, :] = jnp.broadcast_to(jnp.exp(tot), (SUBLANES, D_A))
                e_neg = jnp.exp(-cum)
                at_s[w0_slot, bi, rows, :] = (-kk * jnp.exp(cum - ld)).astype(BF16)
                rt_s[w0_slot, bi, rows, :] = (r * jnp.exp(cum)).astype(BF16)
                bt_s[w0_slot, bi, rows, :] = (kk * a * e_neg).astype(BF16)
                kt_s[w0_slot, bi, rows, :] = (k_d * e_neg).astype(BF16)
                yield

                if finalize:
                    a0f, k_af = pv2_ref[0:1, :], pv2_ref[1:2, :]
                    r_kf, r_kb = pv2_ref[2:3, :], pv2_ref[3:4, :]
                    a_f = _sigmoid(a0f + _dot(ad, aupf_ref[...]))
                    k_df = k * (1.0 + (a_f - 1.0) * k_af)
                    bon_s[w0_slot, bi, rows, :] = _head_sum(r * (k_df * r_kf + k_d * r_kb), ones_ref[...]) * v
                    yield

    prep_ticks = nbat * n_chunks * (4 if finalize else 3)

    n_sq = int(math.log2(C)) - 2
    waves = [(bi, c0) for bi in range(nbat) for c0 in range(0, n_chunks, WAVE_CHUNKS)]

    def matrix_stream():
        for bi, c0 in waves:
            chains = [(c, g) for c in range(c0, c0 + WAVE_CHUNKS) for g in range(N_GROUPS)]
            bd = bd_ref[...]

            def ld_(ref, c, g):
                return ref[r1_slot, bi, c * C:(c + 1) * C, g * G:(g + 1) * G]

            def st_(ref, c, g, val):
                ref[w1_slot, bi, c * C:(c + 1) * C, g * G:(g + 1) * G] = val

            ar = [jnp.concatenate([ld_(at_s, c, g), ld_(rt_s, c, g)], axis=0) for c, g in chains]
            a_b = [_dot_nt(ar[n], _block_diag(ld_(bt_s, c, g), bd)) for n, (c, g) in enumerate(chains)]
            a_k = [_dot_nt(ar[n], _block_diag(ld_(kt_s, c, g), bd)) for n, (c, g) in enumerate(chains)]
            yield
            a_ab = [jnp.where(m_strict, x[0:C], 0.0) for x in a_b]
            for n, (c, g) in enumerate(chains):
                st_(arb_s, c, g, jnp.where(m_incl, a_b[n][C:2 * C], 0.0).astype(BF16))
            a_kk = [jnp.concatenate([jnp.where(m_strict, x[0:C], 0.0), jnp.where(m_incl, x[C:2 * C], 0.0)],
                                    axis=0).astype(BF16) for x in a_k]
            pb = [x.astype(BF16) for x in a_ab]
            p = [_dot(x, _block_diag(x, bd)) for x in pb]
            av = [_dot(a_kk[n], _block_diag(ld_(vb_s, c, g), bd)) for n, (c, g) in enumerate(chains)]
            t = [eye + x for x in a_ab]
            yield
            for n, (c, g) in enumerate(chains):
                st_(avk_s, c, g, av[n][0:C])
                st_(y0_s, c, g, av[n][C:2 * C])
            for _ in range(n_sq):
                pb = [x.astype(BF16) for x in p]
                pt = [_dot(jnp.concatenate([pb[n], t[n].astype(BF16)], axis=0), _block_diag(pb[n], bd))
                      for n in range(len(chains))]
                p = [x[0:C] for x in pt]
                t = [t[n] + pt[n][C:2 * C] for n in range(len(chains))]
                yield
            for n, (c, g) in enumerate(chains):
                tf = t[n] + _dot(t[n].astype(BF16), _block_diag(p[n].astype(BF16), bd))
                st_(tt_s, c, g, tf.astype(BF16))
            yield

    matrix_ticks = len(waves) * (n_sq + 3)

    def state_stream(bi):
        if zero_init:
            s_scr[bi] = jnp.where(first2, 0.0, s_scr[bi])
        else:
            s_scr[bi] = jnp.where(first2, s0_ref[bi], s_scr[bi])
        yield
        sls = [slice(g * G, (g + 1) * G) for g in range(N_GROUPS)]
        for ci in range(n_chunks):
            c = (n_chunks - 1 - ci) if reverse else ci
            rows = slice(c * C, (c + 1) * C)
            bd = bd_ref[...]
            s_c = [s_scr[bi, :, sl] for sl in sls]
            ar = [jnp.concatenate([at_s[r2_slot, bi, rows, sl], rt_s[r2_slot, bi, rows, sl]], axis=0) for sl in sls]
            xs_ = [_dot_nt(ar[g], _block_diag(s_c[g].astype(BF16), bd)) for g in range(N_GROUPS)]
            yield
            w = [xs_[g][0:C] + avk_s[q2_slot, bi, rows, sl] for g, sl in enumerate(sls)]
            u = [_dot(tt_s[q2_slot, bi, rows, sl], _block_diag(w[g].astype(BF16), bd)) for g, sl in enumerate(sls)]
            yield
            ub = [x.astype(BF16) for x in u]
            full = [_dot_tn(jnp.concatenate([ub[g], vb_s[r2_slot, bi, rows, sl]], axis=0),
                            jnp.concatenate([bt_s[r2_slot, bi, rows, sl], kt_s[r2_slot, bi, rows, sl]], axis=0))
                    for g, sl in enumerate(sls)]
            yv = [_dot(arb_s[q2_slot, bi, rows, sl], _block_diag(ub[g], bd)) for g, sl in enumerate(sls)]
            for g, sl in enumerate(sls):
                upd = jnp.zeros((HEAD, G), F32)
                for h in range(HEADS_PER_GROUP):
                    upd = jnp.where(hcol == h, full[g][h * HEAD:(h + 1) * HEAD, :], upd)
                g_tot = gt_s[r2_slot, bi, c * SUBLANES:c * SUBLANES + 1, sl]
                s_scr[bi, :, sl] = (s_c[g] + upd) * g_tot
                yd_ref[bi, rows, sl] = xs_[g][C:2 * C] + yv[g] + y0_s[q2_slot, bi, rows, sl]
            yield
        if finalize:
            gn_w, gn_b = pv2_ref[4:5, :], pv2_ref[5:6, :]
            for hlf in range(2):
                rows = slice(hlf * (tb // 2), (hlf + 1) * (tb // 2))
                yt = yf_ref[bi, rows, :] + yd_ref[bi, rows, :]
                mean = _head_sum(yt, ones_ref[...]) * (1.0 / HEAD)
                yc = yt - mean
                var = _head_sum(yc * yc, ones_ref[...]) * (1.0 / HEAD)
                yn = yc * lax.rsqrt(var + GN_EPS) * gn_w + gn_b + bon_s[r2_slot, bi, rows, :]
                z = za_ref[bi, rows, :]
                y_ref[bi, rows, :] = (yn * (z * _sigmoid(z))).astype(y_ref.dtype)
                yield

    state_ticks = 1 + 3 * n_chunks + (2 if finalize else 0)

    _interleave([(prep_stream(), prep_ticks), (matrix_stream(), matrix_ticks)]
                + [(state_stream(bi), state_ticks) for bi in range(nbat)])

    if emit_state:
        @pl.when(last2)
        def _():
            sout_ref[...] = s_scr[...]


def _scan(rkv, lora, s0c, consts, *, reverse, finalize, emit_state, nbat, tb, yf=None, za=None):
    nb, L, _ = rkv.shape
    nj = L // tb
    n_items = (nb // nbat) * nj
    hb = tb // HALO
    nh = L // HALO
    zero_init = s0c is None
    d = 1 if reverse else 0

    def pos(item):
        jpos = lax.rem(item, nj)
        return item // nj, ((nj - 1 - jpos) if reverse else jpos)

    def item0(i):
        return jnp.minimum(i, n_items - 1)

    def item2(i):
        return jnp.clip(i - 2, 0, n_items - 1)

    def tok(w, item_of):
        def imap(i):
            p, jj = pos(item_of(i))
            return (p, jj, 0)
        return pl.BlockSpec((nbat, tb, w), imap)

    def halo_prev(i):
        p, jj = pos(item0(i))
        return (p, jnp.maximum(jj * hb - 1, 0), 0)

    def halo_next(i):
        p, jj = pos(item0(i))
        return (p, jnp.minimum((jj + 1) * hb, nh - 1), 0)

    full2 = lambda a: pl.BlockSpec(a.shape, lambda i: (0, 0))
    st_spec = pl.BlockSpec((nbat, HEAD, D_A), lambda i: (pos(item2(i))[0], 0, 0))

    args = [rkv, rkv, rkv, lora]
    in_specs = [tok(D_RKV, item0),
                pl.BlockSpec((nbat, HALO, D_RKV), halo_prev),
                pl.BlockSpec((nbat, HALO, D_RKV), halo_next),
                tok(D_LORA, item0)]
    if not zero_init:
        args.append(s0c)
        in_specs.append(st_spec)
    small = [consts["mu"], consts["pv"][d], consts["wup"][d], consts["aup"][d],
             consts["tri"][d], consts["ones_bd"], consts["bd"]]
    args += small
    in_specs += [full2(a) for a in small]
    if finalize:
        args += [yf, za, consts["pv2"], consts["aup"][0]]
        in_specs += [tok(D_A, item2), tok(D_A, item2), full2(consts["pv2"]), full2(consts["aup"][0])]

    out_shape = [jax.ShapeDtypeStruct((nb, L, D_A), BF16 if finalize else F32)]
    out_specs = [tok(D_A, item2)]
    if emit_state:
        out_shape.append(jax.ShapeDtypeStruct((nb, HEAD, D_A), F32))
        out_specs.append(st_spec)

    tokbuf = lambda slots, dt: pltpu.VMEM((slots, nbat, tb, D_A), dt)
    scratch = [pltpu.VMEM((nbat, HEAD, D_A), F32)]
    scratch += [tokbuf(3, BF16)] * 5
    scratch += [pltpu.VMEM((3, nbat, (tb // SCAN_CHUNK) * SUBLANES, D_A), F32)]
    if finalize:
        scratch += [tokbuf(3, F32)]
    scratch += [tokbuf(2, BF16)] * 2 + [tokbuf(2, F32)] * 2
    if finalize:
        scratch += [pltpu.VMEM((nbat, tb, D_A), F32)]

    kern = functools.partial(_scan_kernel, reverse=reverse, finalize=finalize, zero_init=zero_init,
                             emit_state=emit_state, nbat=nbat, tb=tb, nj=nj, n_items=n_items)
    outs = pl.pallas_call(
        kern,
        grid=(n_items + 2,),
        in_specs=in_specs,
        out_specs=out_specs,
        out_shape=out_shape,
        scratch_shapes=scratch,
        compiler_params=pltpu.CompilerParams(
            dimension_semantics=("arbitrary",), vmem_limit_bytes=VMEM_LIMIT),
        name="scan_bwd" if reverse else "scan_fwd",
    )(*args)
    return outs if emit_state else (outs[0], None)


def _out_kernel(x_ref, uvz_ref, ya_ref, mod_ref, lnpost_ref, sgu_ref, ws_ref, bs_ref, wout_ref, o_ref, *, tm):
    gate = mod_ref[0, 2:3, :]
    ln_g = sgu_ref[0:1, :]
    ln_b = sgu_ref[1:2, :]
    lane = lax.broadcasted_iota(jnp.int32, (GMLP_CHUNK, 128), 1)
    low = lane < (D_B // H_B)
    for q in range(tm // GMLP_CHUNK):
        rows = slice(q * GMLP_CHUNK, (q + 1) * GMLP_CHUNK)
        u = uvz_ref[0, rows, 0:D_B].astype(F32)
        vb = uvz_ref[0, rows, D_B:2 * D_B].astype(F32)
        zb = uvz_ref[0, rows, 2 * D_B:3 * D_B].astype(F32)
        mu = jnp.mean(vb, axis=-1, keepdims=True)
        vc = vb - mu
        var = jnp.mean(vc * vc, axis=-1, keepdims=True)
        vn = (vc * lax.rsqrt(var + NORM_EPS)) * ln_g + ln_b
        parts = []
        for pr in range(H_B // 2):
            vp = vn[:, pr * 128:(pr + 1) * 128]
            rhs = jnp.concatenate([jnp.where(low, vp, 0.0), jnp.where(low, 0.0, vp)], axis=0)
            parts.append(_dot(ws_ref[pr], rhs.astype(BF16)))
        s = jnp.concatenate(parts, axis=1) + bs_ref[...]
        yb = u * s * (zb * _sigmoid(zb))
        mixed = jnp.concatenate([ya_ref[0, rows, :], yb.astype(BF16)], axis=1)
        out = _dot(mixed, wout_ref[...])
        ms = jnp.mean(out * out, axis=-1, keepdims=True)
        o_ref[0, rows, :] = x_ref[0, rows, :] + (out * lax.rsqrt(ms + NORM_EPS)) * lnpost_ref[...] * gate


def _out_stage(x, uvz, ya, mod, ln_post, sgu, ws_cat, bs_x, w_out_bf, tm):
    nb, L, _ = x.shape
    tok = lambda d: pl.BlockSpec((1, tm, d), lambda b, i: (b, i, 0))
    full = lambda a: pl.BlockSpec(a.shape, lambda b, i: (0,) * a.ndim)
    return pl.pallas_call(
        functools.partial(_out_kernel, tm=tm),
        grid=(nb, L // tm),
        in_specs=[tok(D_MODEL), tok(D_UVZ), tok(D_A),
                  pl.BlockSpec((1, 3, D_MODEL), lambda b, i: (b, 0, 0)),
                  full(ln_post), full(sgu), full(ws_cat), full(bs_x), full(w_out_bf)],
        out_specs=tok(D_MODEL),
        out_shape=jax.ShapeDtypeStruct((nb, L, D_MODEL), F32),
        compiler_params=pltpu.CompilerParams(
            dimension_semantics=("arbitrary", "arbitrary"), vmem_limit_bytes=VMEM_LIMIT),
        name="out_stage",
    )(x, uvz, ya, mod, ln_post, sgu, ws_cat, bs_x, w_out_bf)


def _layer(x, mod, s0f, s0b, lw, consts, *, emit_state, nbat, tb, tm_in, tm_out):
    rkv, za, lora, uvz = _inproj(x, mod, lw["ln_pre"], lw["w_in_bf"], tm_in)
    yf, sf = _scan(rkv, lora, s0f, consts, reverse=False, finalize=False,
                   emit_state=emit_state, nbat=nbat, tb=tb)
    ya, sb = _scan(rkv, lora, s0b, consts, reverse=True, finalize=True,
                   emit_state=emit_state, nbat=nbat, tb=tb, yf=yf, za=za)
    y = _out_stage(x, uvz, ya, mod, lw["ln_post"], lw["sgu"], lw["ws_cat"], lw["bs_x"],
                   lw["w_out_bf"], tm_out)
    return y, sf, sb


def _pad_rows(a, n):
    return jnp.concatenate([a, jnp.zeros((n - a.shape[0],) + a.shape[1:], a.dtype)], axis=0)


def _to_compact(s):
    b = s.shape[0]
    return jnp.transpose(s, (0, 2, 1, 3)).reshape(b, HEAD, D_A)


def _from_compact(sc):
    b = sc.shape[0]
    return jnp.transpose(sc.reshape(b, HEAD, H_A, HEAD), (0, 2, 1, 3))


def kernel(x_prompt, x_sample, c, state_fwd, state_bwd, c_ctx, ln_pre, ln_post, w_mod, b_mod, w_in, ts_mu, w0, w_up, a0, a_up, k_k, k_a, r_k, gn_w, gn_b, sgu_ln_g, sgu_ln_b, w_s, b_s, w_out):
    depth = w_in.shape[0]
    batch, seq, _ = x_prompt.shape
    dec_batch, dec_seq, _ = x_sample.shape
    tb_ctx, tb_lat = seq, 256

    ti = jnp.arange(SCAN_CHUNK)
    tri = [(ti[None, :] <= ti[:, None]).astype(BF16),
           (ti[None, :] >= ti[:, None]).astype(BF16)]
    gi = jnp.arange(GROUP_LANES) // HEAD
    ones_bd = (gi[:, None] == gi[None, :]).astype(BF16)
    bd = ones_bd

    cvecs = _pad_rows(jnp.concatenate([c_ctx[None, :], c], axis=0), 16)

    y_ctx = x_prompt
    y_lat = x_sample
    new_f, new_b = [], []
    for l in range(depth):
        mods = _modulation(cvecs, w_mod[l], b_mod[l][None, :]).reshape(16, 3, D_MODEL)
        mod_ctx = mods[0:1]
        mod_lat = mods[1:1 + dec_batch]

        zpad = jnp.zeros((LORA, D_A), F32)
        wup = [jnp.concatenate([w_up[l, 0], zpad], 0).astype(BF16),
               jnp.concatenate([zpad, w_up[l, 1]], 0).astype(BF16)]
        aup = [jnp.concatenate([a_up[l, 0], zpad], 0).astype(BF16),
               jnp.concatenate([zpad, a_up[l, 1]], 0).astype(BF16)]
        pv = [_pad_rows(jnp.stack([w0[l, d], a0[l, d], k_k[l, d], k_a[l, d]]), 8) for d in range(2)]
        pv2 = _pad_rows(jnp.stack([a0[l, 0], k_a[l, 0], r_k[l, 0].reshape(D_A), r_k[l, 1].reshape(D_A),
                                   gn_w[l], gn_b[l]]), 8)
        consts = {"mu": ts_mu[l], "pv": pv, "pv2": pv2, "wup": wup, "aup": aup,
                  "tri": tri, "ones_bd": ones_bd, "bd": bd}
        lw = {
            "ln_pre": ln_pre[l][None, :], "ln_post": ln_post[l][None, :],
            "w_in_bf": w_in[l].astype(BF16), "w_out_bf": w_out[l].astype(BF16),
            "sgu": jnp.stack([sgu_ln_g[l], sgu_ln_b[l]]),
            "ws_cat": jnp.concatenate([w_s[l, 0::2], w_s[l, 1::2]], axis=2).astype(BF16),
            "bs_x": jnp.repeat(b_s[l].T, D_B // H_B, axis=1),
        }

        y_ctx, sf, sb = _layer(y_ctx, jnp.broadcast_to(mod_ctx, (batch, 3, D_MODEL)), None, None, lw,
                               consts, emit_state=True, nbat=2, tb=tb_ctx, tm_in=seq, tm_out=seq)
        new_f.append(_from_compact(sf))
        new_b.append(_from_compact(sb))

        y_lat, _, _ = _layer(y_lat, mod_lat, _to_compact(state_fwd[:, l]), _to_compact(state_bwd[:, l]),
                             lw, consts, emit_state=False, nbat=2, tb=tb_lat, tm_in=512, tm_out=512)

    y_prompt = y_ctx
    new_state_fwd = jnp.stack(new_f, axis=1).astype(x_prompt.dtype)
    new_state_bwd = jnp.stack(new_b, axis=1).astype(x_prompt.dtype)
    return (y_prompt, y_lat, new_state_fwd, new_state_bwd)
```

```python
import functools
import math

import jax
import jax.numpy as jnp
from jax import lax
from jax.experimental import pallas as pl
from jax.experimental.pallas import tpu as pltpu

F32 = jnp.float32
BF16 = jnp.bfloat16

D_MODEL = 1024
D_A = 512
D_B = 512
HEAD = 64
H_A = D_A // HEAD
H_B = 8
LORA = 64
GMLP_CHUNK = 128
D_RKV = 3 * D_A
D_LORA = 4 * LORA
D_UVZ = 3 * D_B
D_IN = D_RKV + D_A + D_LORA + D_UVZ
NORM_EPS = 1e-6
GN_EPS = 6.4e-4
L2_EPS = 1e-12
DECAY_SCALE = math.exp(-0.5)

SCAN_CHUNK = 64
GROUP_LANES = 256
HEADS_PER_GROUP = GROUP_LANES // HEAD
N_GROUPS = D_A // GROUP_LANES
HALO = 8
SUBLANES = 8
WAVE_CHUNKS = 4
SCAN_BATCH = 2
SCAN_BLOCK = 256
TOKEN_TILE = 512

VMEM_LIMIT = 56 * 1024 * 1024


def _dot(a, b):
    return jnp.dot(a, b, preferred_element_type=F32)


def _dot_nt(a, b):
    return lax.dot_general(a, b, (((1,), (1,)), ((), ())), preferred_element_type=F32)


def _dot_tn(a, b):
    return lax.dot_general(a, b, (((0,), (0,)), ((), ())), preferred_element_type=F32)


def _split2(x):
    hi = x.astype(BF16)
    lo = (x - hi.astype(F32)).astype(BF16)
    return hi, lo


def _sigmoid(x):
    return 1.0 / (1.0 + jnp.exp(-x))


def _mod_kernel(c_ref, w_ref, b_ref, o_ref):
    c = c_ref[...]
    s = c * _sigmoid(c)
    sh, sl = _split2(s)
    wh, wl = _split2(w_ref[...])
    o_ref[...] = _dot(sh, wh) + _dot(sh, wl) + _dot(sl, wh) + b_ref[...]


def _modulation(cvecs, w_mod, b_mod):
    n = cvecs.shape[0]
    nblk = 3
    return pl.pallas_call(
        _mod_kernel,
        grid=(nblk,),
        in_specs=[
            pl.BlockSpec((n, D_MODEL), lambda j: (0, 0)),
            pl.BlockSpec((D_MODEL, D_MODEL), lambda j: (0, j)),
            pl.BlockSpec((1, D_MODEL), lambda j: (0, j)),
        ],
        out_specs=pl.BlockSpec((n, D_MODEL), lambda j: (0, j)),
        out_shape=jax.ShapeDtypeStruct((n, 3 * D_MODEL), F32),
        compiler_params=pltpu.CompilerParams(vmem_limit_bytes=VMEM_LIMIT),
        name="modulation",
    )(cvecs, w_mod, b_mod)


def _inproj_kernel(x_ref, xp_ref, xn_ref, mod_ref, lnpre_ref, mu_ref, w_ref,
                   rkv_ref, za_ref, lora_ref, uvz_ref, *, tm):
    i = pl.program_id(1)
    ni = pl.num_programs(1)
    x = jnp.concatenate([xp_ref[0], x_ref[0], xn_ref[0]], axis=0)
    shift = mod_ref[0, 0:1, :]
    scale = mod_ref[0, 1:2, :]
    gain = lnpre_ref[...] * (1.0 + scale)
    ms = jnp.mean(x * x, axis=-1, keepdims=True)
    h = (x * lax.rsqrt(ms + NORM_EPS)) * gain + shift
    hb_all = h.astype(BF16)
    hb = hb_all[HALO:HALO + tm]

    o = D_RKV
    za_ref[0] = _dot(hb, w_ref[:, o:o + D_A]); o += D_A
    lora_ref[0] = _dot(hb, w_ref[:, o:o + D_LORA]); o += D_LORA
    uvz_ref[0] = _dot(hb, w_ref[:, o:o + D_UVZ]).astype(uvz_ref.dtype)

    rkv = _dot(hb_all, w_ref[:, 0:D_RKV])
    row = lax.broadcasted_iota(jnp.int32, (tm, D_RKV), 0)
    cur = rkv[HALO:HALO + tm]
    prv = rkv[HALO - 1:HALO - 1 + tm]
    nxt = rkv[HALO + 1:HALO + 1 + tm]
    prv = jnp.where(jnp.logical_and(row == 0, i == 0), 0.0, prv)
    nxt = jnp.where(jnp.logical_and(row == tm - 1, i == ni - 1), 0.0, nxt)
    mu0 = mu_ref[0:1, :]
    mu1 = mu_ref[1:2, :]
    rkv_ref[0] = (1.0 - mu0 - mu1) * cur + mu0 * prv + mu1 * nxt


def _inproj(x, mod, ln_pre, ts_mu, w_in_bf, tm):
    nb, L, _ = x.shape
    hb = tm // HALO
    nh = L // HALO
    tok = lambda d: pl.BlockSpec((1, tm, d), lambda b, i: (b, i, 0))
    shp = lambda d, dt=F32: jax.ShapeDtypeStruct((nb, L, d), dt)
    return pl.pallas_call(
        functools.partial(_inproj_kernel, tm=tm),
        grid=(nb, L // tm),
        in_specs=[
            tok(D_MODEL),
            pl.BlockSpec((1, HALO, D_MODEL), lambda b, i: (b, jnp.maximum(i * hb - 1, 0), 0)),
            pl.BlockSpec((1, HALO, D_MODEL), lambda b, i: (b, jnp.minimum((i + 1) * hb, nh - 1), 0)),
            pl.BlockSpec((1, 3, D_MODEL), lambda b, i: (b, 0, 0)),
            pl.BlockSpec((1, D_MODEL), lambda b, i: (0, 0)),
            pl.BlockSpec((2, D_RKV), lambda b, i: (0, 0)),
            pl.BlockSpec((D_MODEL, D_IN), lambda b, i: (0, 0)),
        ],
        out_specs=[tok(D_RKV), tok(D_A), tok(D_LORA), tok(D_UVZ)],
        out_shape=[shp(D_RKV), shp(D_A), shp(D_LORA), shp(D_UVZ, BF16)],
        compiler_params=pltpu.CompilerParams(
            dimension_semantics=("arbitrary", "arbitrary"), vmem_limit_bytes=VMEM_LIMIT),
        name="inproj",
    )(x, x, x, mod, ln_pre, ts_mu, w_in_bf)


def _block_diag(x, low):
    z = jnp.zeros((x.shape[0], 128), x.dtype)
    blocks = []
    for h in range(HEADS_PER_GROUP):
        tile = x[:, (h // 2) * 128:(h // 2 + 1) * 128]
        kept = jnp.where(low, tile, 0) if h % 2 == 0 else jnp.where(low, 0, tile)
        blocks.append(jnp.concatenate([kept, z] if h < 2 else [z, kept], axis=1))
    return jnp.concatenate(blocks, axis=0)


def _head_sum(x, ones_bd):
    rows = x.shape[0]
    xb = x.astype(BF16)
    parts = [xb[:, g * GROUP_LANES:(g + 1) * GROUP_LANES] for g in range(N_GROUPS)]
    s = _dot(jnp.concatenate(parts, axis=0), ones_bd)
    return jnp.concatenate([s[g * rows:(g + 1) * rows] for g in range(N_GROUPS)], axis=1)


def _interleave(threads):
    gens = [g for g, _ in threads]
    total = [n for _, n in threads]
    done = [0] * len(gens)
    alive = [True] * len(gens)
    while any(alive):
        k = min((i for i in range(len(gens)) if alive[i]), key=lambda i: done[i] / total[i])
        try:
            next(gens[k])
            done[k] += 1
        except StopIteration:
            alive[k] = False


def _scan_kernel(*refs, reverse, finalize, zero_init, emit_state, nbat, tb, nj, n_items):
    it = iter(refs)
    rkv_ref, lora_ref = next(it), next(it)
    s0_ref = None if zero_init else next(it)
    pv_ref, wup_ref, aup_ref = next(it), next(it), next(it)
    tri_ref, ones_ref = next(it), next(it)
    if finalize:
        yf_ref, za_ref, pv2_ref, aupf_ref = next(it), next(it), next(it), next(it)
    y_ref = next(it)
    sout_ref = next(it) if emit_state else None
    s_scr = next(it)
    at_s, rt_s, bt_s, kt_s, vb_s, gt_s = (next(it) for _ in range(6))
    bon_s = next(it) if finalize else None
    tt_s, arb_s, avk_s, y0_s = (next(it) for _ in range(4))
    yd_ref = next(it) if finalize else y_ref

    C = SCAN_CHUNK
    G = GROUP_LANES
    n_chunks = tb // C

    i = pl.program_id(0)
    w0_slot = lax.rem(i, 3)
    r1_slot = lax.rem(i + 2, 3)
    r2_slot = lax.rem(i + 1, 3)
    w1_slot = lax.rem(i + 1, 2)
    q2_slot = lax.rem(i, 2)
    item2 =jnp.clip(i - 2, 0, n_items - 1)
    jpos2 = lax.rem(item2, nj)
    first2 = jpos2 == 0
    last2 = jnp.logical_and(jpos2 == nj - 1, i >= 2)

    @pl.when(i == 0)
    def _():
        for ref in (at_s, rt_s, bt_s, kt_s, vb_s, gt_s, tt_s, arb_s, avk_s, y0_s) + ((bon_s,) if finalize else ()):
            ref[...] = jnp.zeros_like(ref)
        s_scr[...] = jnp.zeros_like(s_scr)

    row = lax.broadcasted_iota(jnp.int32, (C, G), 0)
    col = lax.broadcasted_iota(jnp.int32, (C, G), 1)
    scol = col % HEAD
    hcol = col // HEAD
    if reverse:
        m_strict, m_incl = scol > row, scol >= row
    else:
        m_strict, m_incl = scol < row, scol <= row
    eye = jnp.where(scol == row, 1.0, 0.0).astype(F32)
    last_row = 0 if reverse else C - 1
    low = lax.broadcasted_iota(jnp.int32, (C, 128), 1) < HEAD

    w0, a0, k_k, k_a = pv_ref[0:1, :], pv_ref[1:2, :], pv_ref[2:3, :], pv_ref[3:4, :]

    def prep_stream():
        for bi in range(nbat):
            for c in range(n_chunks):
                rows = slice(c * C, (c + 1) * C)
                r = rkv_ref[bi, rows, 0:D_A]
                k = rkv_ref[bi, rows, D_A:2 * D_A]
                v = rkv_ref[bi, rows, 2 * D_A:3 * D_A]
                vb_s[w0_slot, bi, rows, :] = v.astype(BF16)
                lo = lora_ref[bi, rows, :]
                wd = jnp.tanh(lo[:, 0:2 * LORA]).astype(BF16)
                ad = lo[:, 2 * LORA:4 * LORA].astype(BF16)
                ld = -DECAY_SCALE * _sigmoid(w0 + _dot(wd, wup_ref[...]))
                a = _sigmoid(a0 + _dot(ad, aup_ref[...]))
                kk = k * k_k
                kk2 = _head_sum(kk * kk, ones_ref[...])
                hi, lo2 = _split2(ld)
                cum = _dot(tri_ref[...], jnp.concatenate([hi, lo2], axis=1))
                yield

                cum = cum[:, 0:D_A] + cum[:, D_A:2 * D_A]
                kk = kk * lax.rsqrt(kk2 + L2_EPS)
                k_d = k * (1.0 + (a - 1.0) * k_a)
                tot = cum[last_row:last_row + 1, :]
                gt_s[w0_slot, bi, c * SUBLANES:(c + 1) * SUBLANES, :] = jnp.broadcast_to(jnp.exp(tot), (SUBLANES, D_A))
                e_neg = jnp.exp(-cum)
                at_s[w0_slot, bi, rows, :] = (-kk * jnp.exp(cum - ld)).astype(BF16)
                rt_s[w0_slot, bi, rows, :] = (r * jnp.exp(cum)).astype(BF16)
                bt_s[w0_slot, bi, rows, :] = (kk * a * e_neg).astype(BF16)
                kt_s[w0_slot, bi, rows, :] = (k_d * e_neg).astype(BF16)
                yield

                if finalize:
                    a0f, k_af = pv2_ref[0:1, :], pv2_ref[1:2, :]
                    r_kf, r_kb = pv2_ref[2:3, :], pv2_ref[3:4, :]
                    a_f = _sigmoid(a0f + _dot(ad, aupf_ref[...]))
                    k_df = k * (1.0 + (a_f - 1.0) * k_af)
                    bon_s[w0_slot, bi, rows, :] = _head_sum(r * (k_df * r_kf + k_d * r_kb), ones_ref[...]) * v
                    yield

    prep_ticks = nbat * n_chunks * (3 if finalize else 2)

    n_sq = int(math.log2(C)) - 2
    waves = [(bi, c0) for bi in range(nbat) for c0 in range(0, n_chunks, WAVE_CHUNKS)]

    def matrix_stream():
        for bi, c0 in waves:
            chains = [(c, g) for c in range(c0, c0 + WAVE_CHUNKS) for g in range(N_GROUPS)]

            def ld_(ref, c, g):
                return ref[r1_slot, bi, c * C:(c + 1) * C, g * G:(g + 1) * G]

            def st_(ref, c, g, val):
                ref[w1_slot, bi, c * C:(c + 1) * C, g * G:(g + 1) * G] = val

            ar = [jnp.concatenate([ld_(at_s, c, g), ld_(rt_s, c, g)], axis=0) for c, g in chains]
            a_b = [_dot_nt(ar[n], _block_diag(ld_(bt_s, c, g), low)) for n, (c, g) in enumerate(chains)]
            a_k = [_dot_nt(ar[n], _block_diag(ld_(kt_s, c, g), low)) for n, (c, g) in enumerate(chains)]
            yield
            a_ab = [jnp.where(m_strict, x[0:C], 0.0) for x in a_b]
            for n, (c, g) in enumerate(chains):
                st_(arb_s, c, g, jnp.where(m_incl, a_b[n][C:2 * C], 0.0).astype(BF16))
            a_kk = [jnp.concatenate([jnp.where(m_strict, x[0:C], 0.0), jnp.where(m_incl, x[C:2 * C], 0.0)],
                                    axis=0).astype(BF16) for x in a_k]
            pb = [x.astype(BF16) for x in a_ab]
            p = [_dot(x, _block_diag(x, low)) for x in pb]
            av = [_dot(a_kk[n], _block_diag(ld_(vb_s, c, g), low)) for n, (c, g) in enumerate(chains)]
            t = [eye + x for x in a_ab]
            yield
            for n, (c, g) in enumerate(chains):
                st_(avk_s, c, g, av[n][0:C])
                st_(y0_s, c, g, av[n][C:2 * C])
            for _ in range(n_sq):
                pb = [x.astype(BF16) for x in p]
                pt = [_dot(jnp.concatenate([pb[n], t[n].astype(BF16)], axis=0), _block_diag(pb[n], low))
                      for n in range(len(chains))]
                p = [x[0:C] for x in pt]
                t = [t[n] + pt[n][C:2 * C] for n in range(len(chains))]
                yield
            for n, (c, g) in enumerate(chains):
                tf = t[n] + _dot(t[n].astype(BF16), _block_diag(p[n].astype(BF16), low))
                st_(tt_s, c, g, tf.astype(BF16))
            yield

    matrix_ticks = len(waves) * (n_sq + 3)

    def state_stream(bi):
        if zero_init:
            s_scr[bi] = jnp.where(first2, 0.0, s_scr[bi])
        else:
            s_scr[bi] = jnp.where(first2, s0_ref[bi], s_scr[bi])
        yield
        sls = [slice(g * G, (g + 1) * G) for g in range(N_GROUPS)]
        for ci in range(n_chunks):
            c = (n_chunks - 1 - ci) if reverse else ci
            rows = slice(c * C, (c + 1) * C)
            s_c = [s_scr[bi, :, sl] for sl in sls]
            ar = [jnp.concatenate([at_s[r2_slot, bi, rows, sl], rt_s[r2_slot, bi, rows, sl]], axis=0) for sl in sls]
            xs_ = [_dot_nt(ar[g], _block_diag(s_c[g].astype(BF16), low)) for g in range(N_GROUPS)]
            yield
            w = [xs_[g][0:C] + avk_s[q2_slot, bi, rows, sl] for g, sl in enumerate(sls)]
            u = [_dot(tt_s[q2_slot, bi, rows, sl], _block_diag(w[g].astype(BF16), low)) for g, sl in enumerate(sls)]
            yield
            ub = [x.astype(BF16) for x in u]
            full = [_dot_tn(jnp.concatenate([ub[g], vb_s[r2_slot, bi, rows, sl]], axis=0),
                            jnp.concatenate([bt_s[r2_slot, bi, rows, sl], kt_s[r2_slot, bi, rows, sl]], axis=0))
                    for g, sl in enumerate(sls)]
            yv = [_dot(arb_s[q2_slot, bi, rows, sl], _block_diag(ub[g], low)) for g, sl in enumerate(sls)]
            for g, sl in enumerate(sls):
                upd = jnp.zeros((HEAD, G), F32)
                for h in range(HEADS_PER_GROUP):
                    upd = jnp.where(hcol == h, full[g][h * HEAD:(h + 1) * HEAD, :], upd)
                g_tot = gt_s[r2_slot, bi, c * SUBLANES:c * SUBLANES + 1, sl]
                s_scr[bi, :, sl] = (s_c[g] + upd) * g_tot
                yd_ref[bi, rows, sl] = xs_[g][C:2 * C] + yv[g] + y0_s[q2_slot, bi, rows, sl]
            yield
        if finalize:
            gn_w, gn_b = pv2_ref[4:5, :], pv2_ref[5:6, :]
            for hlf in range(2):
                rows = slice(hlf * (tb // 2), (hlf + 1) * (tb // 2))
                yt = yf_ref[bi, rows, :] + yd_ref[bi, rows, :]
                mean = _head_sum(yt, ones_ref[...]) * (1.0 / HEAD)
                yc = yt - mean
                var = _head_sum(yc * yc, ones_ref[...]) * (1.0 / HEAD)
                yn = yc * lax.rsqrt(var + GN_EPS) * gn_w + gn_b + bon_s[r2_slot, bi, rows, :]
                z = za_ref[bi, rows, :]
                y_ref[bi, rows, :] = (yn * (z * _sigmoid(z))).astype(y_ref.dtype)
                yield

    state_ticks = 1 + 3 * n_chunks + (2 if finalize else 0)

    _interleave([(prep_stream(), prep_ticks), (matrix_stream(), matrix_ticks)]
                + [(state_stream(bi), state_ticks) for bi in range(nbat)])

    if emit_state:
        @pl.when(last2)
        def _():
            sout_ref[...] = s_scr[...]


def _scan(rkv, lora, s0c, consts, *, reverse, finalize, emit_state, nbat, tb, yf=None, za=None):
    nb, L, _ = rkv.shape
    nj = L // tb
    n_items = (nb // nbat) * nj
    zero_init = s0c is None
    d = 1 if reverse else 0

    def pos(item):
        jpos = lax.rem(item, nj)
        return item // nj, ((nj - 1 - jpos) if reverse else jpos)

    def item0(i):
        return jnp.minimum(i, n_items - 1)

    def item2(i):
        return jnp.clip(i - 2, 0, n_items - 1)

    def tok(w, item_of):
        def imap(i):
            p, jj = pos(item_of(i))
            return (p, jj, 0)
        return pl.BlockSpec((nbat, tb, w), imap)

    full2 = lambda a: pl.BlockSpec(a.shape, lambda i: (0, 0))
    st_spec = pl.BlockSpec((nbat, HEAD, D_A), lambda i: (pos(item2(i))[0], 0, 0))

    args = [rkv, lora]
    in_specs = [tok(D_RKV, item0), tok(D_LORA, item0)]
    if not zero_init:
        args.append(s0c)
        in_specs.append(st_spec)
    small = [consts["pv"][d], consts["wup"][d], consts["aup"][d], consts["tri"][d], consts["ones_bd"]]
    args += small
    in_specs += [full2(a) for a in small]
    if finalize:
        args += [yf, za, consts["pv2"], consts["aup"][0]]
        in_specs += [tok(D_A, item2), tok(D_A, item2), full2(consts["pv2"]), full2(consts["aup"][0])]

    out_shape = [jax.ShapeDtypeStruct((nb, L, D_A), BF16 if finalize else F32)]
    out_specs = [tok(D_A, item2)]
    if emit_state:
        out_shape.append(jax.ShapeDtypeStruct((nb, HEAD, D_A), F32))
        out_specs.append(st_spec)

    tokbuf = lambda slots, dt: pltpu.VMEM((slots, nbat, tb, D_A), dt)
    scratch = [pltpu.VMEM((nbat, HEAD, D_A), F32)]
    scratch += [tokbuf(3, BF16)] * 5
    scratch += [pltpu.VMEM((3, nbat, (tb // SCAN_CHUNK) * SUBLANES, D_A), F32)]
    if finalize:
        scratch += [tokbuf(3, F32)]
    scratch += [tokbuf(2, BF16)] * 2 + [tokbuf(2, F32)] * 2
    if finalize:
        scratch += [pltpu.VMEM((nbat, tb, D_A), F32)]

    kern = functools.partial(_scan_kernel, reverse=reverse, finalize=finalize, zero_init=zero_init,
                             emit_state=emit_state, nbat=nbat, tb=tb, nj=nj, n_items=n_items)
    outs = pl.pallas_call(
        kern,
        grid=(n_items + 2,),
        in_specs=in_specs,
        out_specs=out_specs,
        out_shape=out_shape,
        scratch_shapes=scratch,
        compiler_params=pltpu.CompilerParams(
            dimension_semantics=("arbitrary",), vmem_limit_bytes=VMEM_LIMIT),
        name="scan_bwd" if reverse else "scan_fwd",
    )(*args)
    return outs if emit_state else (outs[0], None)


def _out_kernel(x_ref, uvz_ref, ya_ref, mod_ref, lnpost_ref, sgu_ref, ws_ref, bs_ref, wout_ref, o_ref, *, tm):
    gate = mod_ref[0, 2:3, :]
    ln_g = sgu_ref[0:1, :]
    ln_b = sgu_ref[1:2, :]
    lane = lax.broadcasted_iota(jnp.int32, (GMLP_CHUNK, 128), 1)
    low = lane < (D_B // H_B)
    for q in range(tm // GMLP_CHUNK):
        rows = slice(q * GMLP_CHUNK, (q + 1) * GMLP_CHUNK)
        u = uvz_ref[0, rows, 0:D_B].astype(F32)
        vb = uvz_ref[0, rows, D_B:2 * D_B].astype(F32)
        zb = uvz_ref[0, rows, 2 * D_B:3 * D_B].astype(F32)
        mu = jnp.mean(vb, axis=-1, keepdims=True)
        vc = vb - mu
        var = jnp.mean(vc * vc, axis=-1, keepdims=True)
        vn = (vc * lax.rsqrt(var + NORM_EPS)) * ln_g + ln_b
        parts = []
        for pr in range(H_B // 2):
            vp = vn[:, pr * 128:(pr + 1) * 128]
            rhs = jnp.concatenate([jnp.where(low, vp, 0.0), jnp.where(low, 0.0, vp)], axis=0)
            parts.append(_dot(ws_ref[pr], rhs.astype(BF16)))
        s = jnp.concatenate(parts, axis=1) + bs_ref[...]
        yb = u * s * (zb * _sigmoid(zb))
        mixed = jnp.concatenate([ya_ref[0, rows, :], yb.astype(BF16)], axis=1)
        out = _dot(mixed, wout_ref[...])
        ms = jnp.mean(out * out, axis=-1, keepdims=True)
        o_ref[0, rows, :] = x_ref[0, rows, :] + (out * lax.rsqrt(ms + NORM_EPS)) * lnpost_ref[...] * gate


def _out_stage(x, uvz, ya, mod, ln_post, sgu, ws_cat, bs_x, w_out_bf, tm):
    nb, L, _ = x.shape
    tok = lambda d: pl.BlockSpec((1, tm, d), lambda b, i: (b, i, 0))
    full = lambda a: pl.BlockSpec(a.shape, lambda b, i: (0,) * a.ndim)
    return pl.pallas_call(
        functools.partial(_out_kernel, tm=tm),
        grid=(nb, L // tm),
        in_specs=[tok(D_MODEL), tok(D_UVZ), tok(D_A),
                  pl.BlockSpec((1, 3, D_MODEL), lambda b, i: (b, 0, 0)),
                  full(ln_post), full(sgu), full(ws_cat), full(bs_x), full(w_out_bf)],
        out_specs=tok(D_MODEL),
        out_shape=jax.ShapeDtypeStruct((nb, L, D_MODEL), F32),
        compiler_params=pltpu.CompilerParams(
            dimension_semantics=("arbitrary", "arbitrary"), vmem_limit_bytes=VMEM_LIMIT),
        name="out_stage",
    )(x, uvz, ya, mod, ln_post, sgu, ws_cat, bs_x, w_out_bf)


def _layer(x, mod, s0f, s0b, lw, consts, *, emit_state, nbat, tb, tm_in, tm_out):
    rkv, za, lora, uvz = _inproj(x, mod, lw["ln_pre"], lw["ts_mu"], lw["w_in_bf"], tm_in)
    yf, sf = _scan(rkv, lora, s0f, consts, reverse=False, finalize=False,
                   emit_state=emit_state, nbat=nbat, tb=tb)
    ya, sb = _scan(rkv, lora, s0b, consts, reverse=True, finalize=True,
                   emit_state=emit_state, nbat=nbat, tb=tb, yf=yf, za=za)
    y = _out_stage(x, uvz, ya, mod, lw["ln_post"], lw["sgu"], lw["ws_cat"], lw["bs_x"],
                   lw["w_out_bf"], tm_out)
    return y, sf, sb


def _pad_rows(a, n):
    return jnp.concatenate([a, jnp.zeros((n - a.shape[0],) + a.shape[1:], a.dtype)], axis=0)


def _to_compact(s):
    b = s.shape[0]
    return jnp.transpose(s, (0, 2, 1, 3)).reshape(b, HEAD, D_A)


def _from_compact(sc):
    b = sc.shape[0]
    return jnp.transpose(sc.reshape(b, HEAD, H_A, HEAD), (0, 2, 1, 3))


def kernel(x_prompt, x_sample, c, state_fwd, state_bwd, c_ctx, ln_pre, ln_post, w_mod, b_mod, w_in, ts_mu, w0, w_up, a0, a_up, k_k, k_a, r_k, gn_w, gn_b, sgu_ln_g, sgu_ln_b, w_s, b_s, w_out):
    depth = w_in.shape[0]
    batch, seq, _ = x_prompt.shape
    dec_batch, dec_seq, _ = x_sample.shape

    ti = jnp.arange(SCAN_CHUNK)
    tri = [(ti[None, :] <= ti[:, None]).astype(BF16),
           (ti[None, :] >= ti[:, None]).astype(BF16)]
    gi = jnp.arange(GROUP_LANES) // HEAD
    ones_bd = (gi[:, None] == gi[None, :]).astype(BF16)

    cvecs = _pad_rows(jnp.concatenate([c_ctx[None, :], c], axis=0), 16)

    y_ctx = x_prompt
    y_lat = x_sample
    new_f, new_b = [], []
    for l in range(depth):
        mods = _modulation(cvecs, w_mod[l], b_mod[l][None, :]).reshape(16, 3, D_MODEL)
        mod_ctx = mods[0:1]
        mod_lat = mods[1:1 + dec_batch]

        zpad = jnp.zeros((LORA, D_A), F32)
        wup = [jnp.concatenate([w_up[l, 0], zpad], 0).astype(BF16),
               jnp.concatenate([zpad, w_up[l, 1]], 0).astype(BF16)]
        aup = [jnp.concatenate([a_up[l, 0], zpad], 0).astype(BF16),
               jnp.concatenate([zpad, a_up[l, 1]], 0).astype(BF16)]
        pv = [_pad_rows(jnp.stack([w0[l, d], a0[l, d], k_k[l, d], k_a[l, d]]), 8) for d in range(2)]
        pv2 = _pad_rows(jnp.stack([a0[l, 0], k_a[l, 0], r_k[l, 0].reshape(D_A), r_k[l, 1].reshape(D_A),
                                   gn_w[l], gn_b[l]]), 8)
        consts = {"pv": pv, "pv2": pv2, "wup": wup, "aup": aup, "tri": tri, "ones_bd": ones_bd}
        lw = {
            "ln_pre": ln_pre[l][None, :], "ln_post": ln_post[l][None, :], "ts_mu": ts_mu[l],
            "w_in_bf": w_in[l].astype(BF16), "w_out_bf": w_out[l].astype(BF16),
            "sgu": jnp.stack([sgu_ln_g[l], sgu_ln_b[l]]),
            "ws_cat": jnp.concatenate([w_s[l, 0::2], w_s[l, 1::2]], axis=2).astype(BF16),
            "bs_x": jnp.repeat(b_s[l].T, D_B // H_B, axis=1),
        }

        y_ctx, sf, sb = _layer(y_ctx, jnp.broadcast_to(mod_ctx, (batch, 3, D_MODEL)), None, None, lw,
                               consts, emit_state=True, nbat=SCAN_BATCH, tb=SCAN_BLOCK, tm_in=seq, tm_out=seq)
        new_f.append(_from_compact(sf))
        new_b.append(_from_compact(sb))

        y_lat, _, _ = _layer(y_lat, mod_lat, _to_compact(state_fwd[:, l]), _to_compact(state_bwd[:, l]),
                             lw, consts, emit_state=False, nbat=SCAN_BATCH, tb=SCAN_BLOCK,
                             tm_in=TOKEN_TILE, tm_out=TOKEN_TILE)

    y_prompt = y_ctx
    new_state_fwd = jnp.stack(new_f, axis=1).astype(x_prompt.dtype)
    new_state_bwd = jnp.stack(new_b, axis=1).astype(x_prompt.dtype)
    return (y_prompt, y_lat, new_state_fwd, new_state_bwd)
```

```python
import functools
import math

import jax
import jax.numpy as jnp
from jax import lax
from jax.experimental import pallas as pl
from jax.experimental.pallas import tpu as pltpu

F32 = jnp.float32
BF16 = jnp.bfloat16

D_MODEL = 1024
D_A = 512
D_B = 512
HEAD = 64
H_A = D_A // HEAD
H_B = 8
LORA = 64
GMLP_CHUNK = 128
D_RKV = 3 * D_A
D_LORA = 4 * LORA
D_UVZ = 3 * D_B
D_IN = D_RKV + D_A + D_LORA + D_UVZ
NORM_EPS = 1e-6
GN_EPS = 6.4e-4
L2_EPS = 1e-12
DECAY_SCALE = math.exp(-0.5)

SCAN_CHUNK = 64
GROUP_LANES = 256
HEADS_PER_GROUP = GROUP_LANES // HEAD
N_GROUPS = D_A // GROUP_LANES
HALO = 8
SUBLANES = 8
WAVE_CHUNKS = 4
SCAN_BATCH = 2
SCAN_BLOCK = 256
TOKEN_TILE = 512

VMEM_LIMIT = 56 * 1024 * 1024


def _dot(a, b):
    return jnp.dot(a, b, preferred_element_type=F32)


def _dot_nt(a, b):
    return lax.dot_general(a, b, (((1,), (1,)), ((), ())), preferred_element_type=F32)


def _dot_tn(a, b):
    return lax.dot_general(a, b, (((0,), (0,)), ((), ())), preferred_element_type=F32)


def _split2(x):
    hi = x.astype(BF16)
    lo = (x - hi.astype(F32)).astype(BF16)
    return hi, lo


def _sigmoid(x):
    return 1.0 / (1.0 + jnp.exp(-x))


def _mod_kernel(c_ref, w_ref, b_ref, o_ref):
    c = c_ref[...]
    s = c * _sigmoid(c)
    sh, sl = _split2(s)
    wh, wl = _split2(w_ref[...])
    o_ref[...] = _dot(sh, wh) + _dot(sh, wl) + _dot(sl, wh) + b_ref[...]


def _modulation(cvecs, w_mod, b_mod):
    n = cvecs.shape[0]
    nblk = 3
    return pl.pallas_call(
        _mod_kernel,
        grid=(nblk,),
        in_specs=[
            pl.BlockSpec((n, D_MODEL), lambda j: (0, 0)),
            pl.BlockSpec((D_MODEL, D_MODEL), lambda j: (0, j)),
            pl.BlockSpec((1, D_MODEL), lambda j: (0, j)),
        ],
        out_specs=pl.BlockSpec((n, D_MODEL), lambda j: (0, j)),
        out_shape=jax.ShapeDtypeStruct((n, 3 * D_MODEL), F32),
        compiler_params=pltpu.CompilerParams(vmem_limit_bytes=VMEM_LIMIT),
        name="modulation",
    )(cvecs, w_mod, b_mod)


def _inproj_kernel(x_ref, xp_ref, xn_ref, mod_ref, lnpre_ref, mu_ref, w_ref,
                   rkv_ref, za_ref, lora_ref, uvz_ref, *, tm):
    i = pl.program_id(1)
    ni = pl.num_programs(1)
    x = jnp.concatenate([xp_ref[0], x_ref[0], xn_ref[0]], axis=0)
    shift = mod_ref[0, 0:1, :]
    scale = mod_ref[0, 1:2, :]
    gain = lnpre_ref[...] * (1.0 + scale)
    ms = jnp.mean(x * x, axis=-1, keepdims=True)
    h = (x * lax.rsqrt(ms + NORM_EPS)) * gain + shift
    hb_all = h.astype(BF16)
    hb = hb_all[HALO:HALO + tm]

    o = D_RKV
    za_ref[0] = _dot(hb, w_ref[:, o:o + D_A]); o += D_A
    lora_ref[0] = _dot(hb, w_ref[:, o:o + D_LORA]); o += D_LORA
    uvz_ref[0] = _dot(hb, w_ref[:, o:o + D_UVZ]).astype(uvz_ref.dtype)

    rkv = _dot(hb_all, w_ref[:, 0:D_RKV])
    row = lax.broadcasted_iota(jnp.int32, (tm, D_RKV), 0)
    cur = rkv[HALO:HALO + tm]
    prv = rkv[HALO - 1:HALO - 1 + tm]
    nxt = rkv[HALO + 1:HALO + 1 + tm]
    prv = jnp.where(jnp.logical_and(row == 0, i == 0), 0.0, prv)
    nxt = jnp.where(jnp.logical_and(row == tm - 1, i == ni - 1), 0.0, nxt)
    mu0 = mu_ref[0:1, :]
    mu1 = mu_ref[1:2, :]
    rkv_ref[0] = (1.0 - mu0 - mu1) * cur + mu0 * prv + mu1 * nxt


def _inproj(x, mod, ln_pre, ts_mu, w_in_bf, tm):
    nb, L, _ = x.shape
    hb = tm // HALO
    nh = L // HALO
    tok = lambda d: pl.BlockSpec((1, tm, d), lambda b, i: (b, i, 0))
    shp = lambda d, dt=F32: jax.ShapeDtypeStruct((nb, L, d), dt)
    return pl.pallas_call(
        functools.partial(_inproj_kernel, tm=tm),
        grid=(nb, L // tm),
        in_specs=[
            tok(D_MODEL),
            pl.BlockSpec((1, HALO, D_MODEL), lambda b, i: (b, jnp.maximum(i * hb - 1, 0), 0)),
            pl.BlockSpec((1, HALO, D_MODEL), lambda b, i: (b, jnp.minimum((i + 1) * hb, nh - 1), 0)),
            pl.BlockSpec((1, 3, D_MODEL), lambda b, i: (b, 0, 0)),
            pl.BlockSpec((1, D_MODEL), lambda b, i: (0, 0)),
            pl.BlockSpec((2, D_RKV), lambda b, i: (0, 0)),
            pl.BlockSpec((D_MODEL, D_IN), lambda b, i: (0, 0)),
        ],
        out_specs=[tok(D_RKV), tok(D_A), tok(D_LORA), tok(D_UVZ)],
        out_shape=[shp(D_RKV), shp(D_A), shp(D_LORA), shp(D_UVZ, BF16)],
        compiler_params=pltpu.CompilerParams(
            dimension_semantics=("arbitrary", "arbitrary"), vmem_limit_bytes=VMEM_LIMIT),
        name="inproj",
    )(x, x, x, mod, ln_pre, ts_mu, w_in_bf)


def _block_diag(x, low):
    z = jnp.zeros((x.shape[0], 128), x.dtype)
    blocks = []
    for h in range(HEADS_PER_GROUP):
        tile = x[:, (h // 2) * 128:(h // 2 + 1) * 128]
        kept = jnp.where(low, tile, 0) if h % 2 == 0 else jnp.where(low, 0, tile)
        blocks.append(jnp.concatenate([kept, z] if h < 2 else [z, kept], axis=1))
    return jnp.concatenate(blocks, axis=0)


def _block_diag_t(xt, low):
    z = jnp.zeros((HEAD, 128), xt.dtype)
    blocks = []
    for h in range(HEADS_PER_GROUP):
        tile = xt[h * HEAD:(h + 1) * HEAD, :]
        kept = jnp.where(low, tile, 0) if h % 2 == 0 else jnp.where(low, 0, tile)
        blocks.append(jnp.concatenate([kept, z] if h < 2 else [z, kept], axis=1))
    return jnp.concatenate(blocks, axis=0)


def _head_sum(x, ones_bd):
    rows = x.shape[0]
    xb = x.astype(BF16)
    parts = [xb[:, g * GROUP_LANES:(g + 1) * GROUP_LANES] for g in range(N_GROUPS)]
    s = _dot(jnp.concatenate(parts, axis=0), ones_bd)
    return jnp.concatenate([s[g * rows:(g + 1) * rows] for g in range(N_GROUPS)], axis=1)


def _interleave(threads):
    gens = [g for g, _ in threads]
    total = [n for _, n in threads]
    done = [0] * len(gens)
    alive = [True] * len(gens)
    while any(alive):
        k = min((i for i in range(len(gens)) if alive[i]), key=lambda i: done[i] / total[i])
        try:
            next(gens[k])
            done[k] += 1
        except StopIteration:
            alive[k] = False


def _scan_kernel(*refs, reverse, finalize, zero_init, emit_state, nbat, tb, nj, n_items):
    it = iter(refs)
    rkv_ref, lora_ref = next(it), next(it)
    s0_ref = None if zero_init else next(it)
    pv_ref, wup_ref, aup_ref = next(it), next(it), next(it)
    tri_ref, ones_ref = next(it), next(it)
    if finalize:
        yf_ref, za_ref, pv2_ref, aupf_ref = next(it), next(it), next(it), next(it)
    y_ref = next(it)
    sout_ref = next(it) if emit_state else None
    s_scr = next(it)
    at_s, rt_s, bt_s, kt_s, vb_s, gt_s = (next(it) for _ in range(6))
    btT_s, ktT_s = next(it), next(it)
    bon_s = next(it) if finalize else None
    tt_s, arb_s, avk_s, y0_s = (next(it) for _ in range(4))
    yd_ref = next(it) if finalize else y_ref

    C = SCAN_CHUNK
    G = GROUP_LANES
    n_chunks = tb // C

    i = pl.program_id(0)
    w0_slot = lax.rem(i, 3)
    r1_slot = lax.rem(i + 2, 3)
    r2_slot = lax.rem(i + 1, 3)
    w1_slot = lax.rem(i + 1, 2)
    q2_slot = lax.rem(i, 2)
    item2 =jnp.clip(i - 2, 0, n_items - 1)
    jpos2 = lax.rem(item2, nj)
    first2 = jpos2 == 0
    last2 = jnp.logical_and(jpos2 == nj - 1, i >= 2)

    @pl.when(i == 0)
    def _():
        for ref in (at_s, rt_s, bt_s, kt_s, vb_s, gt_s, btT_s, ktT_s, tt_s, arb_s, avk_s, y0_s) + (
                (bon_s,) if finalize else ()):
            ref[...] = jnp.zeros_like(ref)
        s_scr[...] = jnp.zeros_like(s_scr)

    row = lax.broadcasted_iota(jnp.int32, (C, G), 0)
    col = lax.broadcasted_iota(jnp.int32, (C, G), 1)
    scol = col % HEAD
    hcol = col // HEAD
    if reverse:
        m_strict, m_incl = scol > row, scol >= row
    else:
        m_strict, m_incl = scol < row, scol <= row
    eye = jnp.where(scol == row, 1.0, 0.0).astype(F32)
    last_row = 0 if reverse else C - 1
    low = lax.broadcasted_iota(jnp.int32, (C, 128), 1) < HEAD

    w0, a0, k_k, k_a = pv_ref[0:1, :], pv_ref[1:2, :], pv_ref[2:3, :], pv_ref[3:4, :]

    def prep_stream():
        for bi in range(nbat):
            for c in range(n_chunks):
                rows = slice(c * C, (c + 1) * C)
                r = rkv_ref[bi, rows, 0:D_A]
                k = rkv_ref[bi, rows, D_A:2 * D_A]
                v = rkv_ref[bi, rows, 2 * D_A:3 * D_A]
                vb_s[w0_slot, bi, rows, :] = v.astype(BF16)
                lo = lora_ref[bi, rows, :]
                wd = jnp.tanh(lo[:, 0:2 * LORA]).astype(BF16)
                ad = lo[:, 2 * LORA:4 * LORA].astype(BF16)
                ld = -DECAY_SCALE * _sigmoid(w0 + _dot(wd, wup_ref[...]))
                a = _sigmoid(a0 + _dot(ad, aup_ref[...]))
                kk = k * k_k
                kk2 = _head_sum(kk * kk, ones_ref[...])
                hi, lo2 = _split2(ld)
                cum = _dot(tri_ref[...], jnp.concatenate([hi, lo2], axis=1))
                yield

                cum = cum[:, 0:D_A] + cum[:, D_A:2 * D_A]
                kk = kk * lax.rsqrt(kk2 + L2_EPS)
                k_d = k * (1.0 + (a - 1.0) * k_a)
                tot = cum[last_row:last_row + 1, :]
                gt_s[w0_slot, bi, c * SUBLANES:(c + 1) * SUBLANES, :] = jnp.broadcast_to(jnp.exp(tot), (SUBLANES, D_A))
                e_neg = jnp.exp(-cum)
                at_s[w0_slot, bi, rows, :] = (-kk * jnp.exp(cum - ld)).astype(BF16)
                rt_s[w0_slot, bi, rows, :] = (r * jnp.exp(cum)).astype(BF16)
                bt = kk * a * e_neg
                kt = k_d * e_neg
                bt_s[w0_slot, bi, rows, :] = bt.astype(BF16)
                kt_s[w0_slot, bi, rows, :] = kt.astype(BF16)
                btT_s[w0_slot, bi, c] = jnp.concatenate([bt, bt], axis=0).T.astype(BF16)
                ktT_s[w0_slot, bi, c] = jnp.concatenate([kt, kt], axis=0).T.astype(BF16)
                yield

                if finalize:
                    a0f, k_af = pv2_ref[0:1, :], pv2_ref[1:2, :]
                    r_kf, r_kb = pv2_ref[2:3, :], pv2_ref[3:4, :]
                    a_f = _sigmoid(a0f + _dot(ad, aupf_ref[...]))
                    k_df = k * (1.0 + (a_f - 1.0) * k_af)
                    bon_s[w0_slot, bi, rows, :] = _head_sum(r * (k_df * r_kf + k_d * r_kb), ones_ref[...]) * v
                    yield

    prep_ticks = nbat * n_chunks * (3 if finalize else 2)

    n_sq = int(math.log2(C)) - 2
    waves = [(bi, c0) for bi in range(nbat) for c0 in range(0, n_chunks, WAVE_CHUNKS)]

    def matrix_stream():
        for bi, c0 in waves:
            chains = [(c, g) for c in range(c0, c0 + WAVE_CHUNKS) for g in range(N_GROUPS)]

            def ld_(ref, c, g):
                return ref[r1_slot, bi, c * C:(c + 1) * C, g * G:(g + 1) * G]

            def st_(ref, c, g, val):
                ref[w1_slot, bi, c * C:(c + 1) * C, g * G:(g + 1) * G] = val

            ar = [jnp.concatenate([ld_(at_s, c, g), ld_(rt_s, c, g)], axis=0) for c, g in chains]
            a_b = [_dot(ar[n], _block_diag_t(btT_s[r1_slot, bi, c, g * G:(g + 1) * G, :], low))
                   for n, (c, g) in enumerate(chains)]
            a_k = [_dot(ar[n], _block_diag_t(ktT_s[r1_slot, bi, c, g * G:(g + 1) * G, :], low))
                   for n, (c, g) in enumerate(chains)]
            yield
            a_ab = [jnp.where(m_strict, x[0:C], 0.0) for x in a_b]
            for n, (c, g) in enumerate(chains):
                st_(arb_s, c, g, jnp.where(m_incl, a_b[n][C:2 * C], 0.0).astype(BF16))
            a_kk = [jnp.concatenate([jnp.where(m_strict, x[0:C], 0.0), jnp.where(m_incl, x[C:2 * C], 0.0)],
                                    axis=0).astype(BF16) for x in a_k]
            pb = [x.astype(BF16) for x in a_ab]
            pb = [_dot(x, _block_diag(x, low)).astype(BF16) for x in pb]
            av = [_dot(a_kk[n], _block_diag(ld_(vb_s, c, g), low)) for n, (c, g) in enumerate(chains)]
            t = [eye + x for x in a_ab]
            yield
            for n, (c, g) in enumerate(chains):
                st_(avk_s, c, g, av[n][0:C])
                st_(y0_s, c, g, av[n][C:2 * C])
            for _ in range(n_sq):
                pt = [_dot(jnp.concatenate([pb[n], t[n].astype(BF16)], axis=0), _block_diag(pb[n], low))
                      for n in range(len(chains))]
                pb = [x[0:C].astype(BF16) for x in pt]
                t = [t[n] + pt[n][C:2 * C] for n in range(len(chains))]
                yield
            for n, (c, g) in enumerate(chains):
                tf = t[n] + _dot(t[n].astype(BF16), _block_diag(pb[n], low))
                st_(tt_s, c, g, tf.astype(BF16))
            yield

    matrix_ticks = len(waves) * (n_sq + 3)

    def state_stream(bi):
        if zero_init:
            s_scr[bi] = jnp.where(first2, 0.0, s_scr[bi])
        else:
            s_scr[bi] = jnp.where(first2, s0_ref[bi], s_scr[bi])
        yield
        sls = [slice(g * G, (g + 1) * G) for g in range(N_GROUPS)]
        for ci in range(n_chunks):
            c = (n_chunks - 1 - ci) if reverse else ci
            rows = slice(c * C, (c + 1) * C)
            s_c = [s_scr[bi, :, sl] for sl in sls]
            ar = [jnp.concatenate([at_s[r2_slot, bi, rows, sl], rt_s[r2_slot, bi, rows, sl]], axis=0) for sl in sls]
            xs_ = [_dot_nt(ar[g], _block_diag(s_c[g].astype(BF16), low)) for g in range(N_GROUPS)]
            yield
            w = [xs_[g][0:C] + avk_s[q2_slot, bi, rows, sl] for g, sl in enumerate(sls)]
            u = [_dot(tt_s[q2_slot, bi, rows, sl], _block_diag(w[g].astype(BF16), low)) for g, sl in enumerate(sls)]
            yield
            ub = [x.astype(BF16) for x in u]
            full = [_dot_tn(jnp.concatenate([ub[g], vb_s[r2_slot, bi, rows, sl]], axis=0),
                            jnp.concatenate([bt_s[r2_slot, bi, rows, sl], kt_s[r2_slot, bi, rows, sl]], axis=0))
                    for g, sl in enumerate(sls)]
            yv = [_dot(arb_s[q2_slot, bi, rows, sl], _block_diag(ub[g], low)) for g, sl in enumerate(sls)]
            for g, sl in enumerate(sls):
                upd = jnp.zeros((HEAD, G), F32)
                for h in range(HEADS_PER_GROUP):
                    upd = jnp.where(hcol == h, full[g][h * HEAD:(h + 1) * HEAD, :], upd)
                g_tot = gt_s[r2_slot, bi, c * SUBLANES:c * SUBLANES + 1, sl]
                s_scr[bi, :, sl] = (s_c[g] + upd) * g_tot
                yd_ref[bi, rows, sl] = xs_[g][C:2 * C] + yv[g] + y0_s[q2_slot, bi, rows, sl]
            yield
        if finalize:
            gn_w, gn_b = pv2_ref[4:5, :], pv2_ref[5:6, :]
            for hlf in range(2):
                rows = slice(hlf * (tb // 2), (hlf + 1) * (tb // 2))
                yt = yf_ref[bi, rows, :] + yd_ref[bi, rows, :]
                mean = _head_sum(yt, ones_ref[...]) * (1.0 / HEAD)
                yc = yt - mean
                var = _head_sum(yc * yc, ones_ref[...]) * (1.0 / HEAD)
                yn = yc * lax.rsqrt(var + GN_EPS) * gn_w + gn_b + bon_s[r2_slot, bi, rows, :]
                z = za_ref[bi, rows, :]
                y_ref[bi, rows, :] = (yn * (z * _sigmoid(z))).astype(y_ref.dtype)
                yield

    state_ticks = 1 + 3 * n_chunks + (2 if finalize else 0)

    _interleave([(prep_stream(), prep_ticks), (matrix_stream(), matrix_ticks)]
                + [(state_stream(bi), state_ticks) for bi in range(nbat)])

    if emit_state:
        @pl.when(last2)
        def _():
            sout_ref[...] = s_scr[...]


def _scan(rkv, lora, s0c, consts, *, reverse, finalize, emit_state, nbat, tb, yf=None, za=None):
    nb, L, _ = rkv.shape
    nj = L // tb
    n_items = (nb // nbat) * nj
    zero_init = s0c is None
    d = 1 if reverse else 0

    def pos(item):
        jpos = lax.rem(item, nj)
        return item // nj, ((nj - 1 - jpos) if reverse else jpos)

    def item0(i):
        return jnp.minimum(i, n_items - 1)

    def item2(i):
        return jnp.clip(i - 2, 0, n_items - 1)

    def tok(w, item_of):
        def imap(i):
            p, jj = pos(item_of(i))
            return (p, jj, 0)
        return pl.BlockSpec((nbat, tb, w), imap)

    full2 = lambda a: pl.BlockSpec(a.shape, lambda i: (0, 0))
    st_spec = pl.BlockSpec((nbat, HEAD, D_A), lambda i: (pos(item2(i))[0], 0, 0))

    args = [rkv, lora]
    in_specs = [tok(D_RKV, item0), tok(D_LORA, item0)]
    if not zero_init:
        args.append(s0c)
        in_specs.append(st_spec)
    small = [consts["pv"][d], consts["wup"][d], consts["aup"][d], consts["tri"][d], consts["ones_bd"]]
    args += small
    in_specs += [full2(a) for a in small]
    if finalize:
        args += [yf, za, consts["pv2"], consts["aup"][0]]
        in_specs += [tok(D_A, item2), tok(D_A, item2), full2(consts["pv2"]), full2(consts["aup"][0])]

    out_shape = [jax.ShapeDtypeStruct((nb, L, D_A), BF16 if finalize else F32)]
    out_specs = [tok(D_A, item2)]
    if emit_state:
        out_shape.append(jax.ShapeDtypeStruct((nb, HEAD, D_A), F32))
        out_specs.append(st_spec)

    tokbuf = lambda slots, dt: pltpu.VMEM((slots, nbat, tb, D_A), dt)
    scratch = [pltpu.VMEM((nbat, HEAD, D_A), F32)]
    scratch += [tokbuf(3, BF16)] * 5
    scratch += [pltpu.VMEM((3, nbat, (tb // SCAN_CHUNK) * SUBLANES, D_A), F32)]
    scratch += [pltpu.VMEM((3, nbat, tb // SCAN_CHUNK, D_A, 2 * SCAN_CHUNK), BF16)] * 2
    if finalize:
        scratch += [tokbuf(3, F32)]
    scratch += [tokbuf(2, BF16)] * 2 + [tokbuf(2, F32)] * 2
    if finalize:
        scratch += [pltpu.VMEM((nbat, tb, D_A), F32)]

    kern = functools.partial(_scan_kernel, reverse=reverse, finalize=finalize, zero_init=zero_init,
                             emit_state=emit_state, nbat=nbat, tb=tb, nj=nj, n_items=n_items)
    outs = pl.pallas_call(
        kern,
        grid=(n_items + 2,),
        in_specs=in_specs,
        out_specs=out_specs,
        out_shape=out_shape,
        scratch_shapes=scratch,
        compiler_params=pltpu.CompilerParams(
            dimension_semantics=("arbitrary",), vmem_limit_bytes=VMEM_LIMIT),
        name="scan_bwd" if reverse else "scan_fwd",
    )(*args)
    return outs if emit_state else (outs[0], None)


def _out_kernel(x_ref, uvz_ref, ya_ref, mod_ref, lnpost_ref, sgu_ref, ws_ref, bs_ref, wout_ref, o_ref, *, tm):
    gate = mod_ref[0, 2:3, :]
    ln_g = sgu_ref[0:1, :]
    ln_b = sgu_ref[1:2, :]
    lane = lax.broadcasted_iota(jnp.int32, (GMLP_CHUNK, 128), 1)
    low = lane < (D_B // H_B)
    post_gain = lnpost_ref[...] * gate

    def chunk_stream(q):
        rows = slice(q * GMLP_CHUNK, (q + 1) * GMLP_CHUNK)
        vb = uvz_ref[0, rows, D_B:2 * D_B].astype(F32)
        mu = jnp.mean(vb, axis=-1, keepdims=True)
        vc = vb - mu
        var = jnp.mean(vc * vc, axis=-1, keepdims=True)
        vn = (vc * lax.rsqrt(var + NORM_EPS)) * ln_g + ln_b
        yield
        parts = []
        for pr in range(H_B // 2):
            vp = vn[:, pr * 128:(pr + 1) * 128]
            rhs = jnp.concatenate([jnp.where(low, vp, 0.0), jnp.where(low, 0.0, vp)], axis=0)
            parts.append(_dot(ws_ref[pr], rhs.astype(BF16)))
        yield
        u = uvz_ref[0, rows, 0:D_B].astype(F32)
        zb = uvz_ref[0, rows, 2 * D_B:3 * D_B].astype(F32)
        s = jnp.concatenate(parts, axis=1) + bs_ref[...]
        yb = u * s * (zb * _sigmoid(zb))
        mixed = jnp.concatenate([ya_ref[0, rows, :], yb.astype(BF16)], axis=1)
        out = _dot(mixed, wout_ref[...])
        yield
        ms = jnp.mean(out * out, axis=-1, keepdims=True)
        o_ref[0, rows, :] = x_ref[0, rows, :] + (out * lax.rsqrt(ms + NORM_EPS)) * post_gain
        yield

    n_q = tm // GMLP_CHUNK
    streams = [chunk_stream(q) for q in range(n_q)]
    for step in range(n_q + 3):
        for q in range(n_q):
            if 0 <= step - q < 4:
                next(streams[q])


def _out_stage(x, uvz, ya, mod, ln_post, sgu, ws_cat, bs_x, w_out_bf, tm):
    nb, L, _ = x.shape
    tok = lambda d: pl.BlockSpec((1, tm, d), lambda b, i: (b, i, 0))
    full = lambda a: pl.BlockSpec(a.shape, lambda b, i: (0,) * a.ndim)
    return pl.pallas_call(
        functools.partial(_out_kernel, tm=tm),
        grid=(nb, L // tm),
        in_specs=[tok(D_MODEL), tok(D_UVZ), tok(D_A),
                  pl.BlockSpec((1, 3, D_MODEL), lambda b, i: (b, 0, 0)),
                  full(ln_post), full(sgu), full(ws_cat), full(bs_x), full(w_out_bf)],
        out_specs=tok(D_MODEL),
        out_shape=jax.ShapeDtypeStruct((nb, L, D_MODEL), F32),
        compiler_params=pltpu.CompilerParams(
            dimension_semantics=("arbitrary", "arbitrary"), vmem_limit_bytes=VMEM_LIMIT),
        name="out_stage",
    )(x, uvz, ya, mod, ln_post, sgu, ws_cat, bs_x, w_out_bf)


def _layer(x, mod, s0f, s0b, lw, consts, *, emit_state, nbat, tb, tm_in, tm_out):
    rkv, za, lora, uvz = _inproj(x, mod, lw["ln_pre"], lw["ts_mu"], lw["w_in_bf"], tm_in)
    yf, sf = _scan(rkv, lora, s0f, consts, reverse=False, finalize=False,
                   emit_state=emit_state, nbat=nbat, tb=tb)
    ya, sb = _scan(rkv, lora, s0b, consts, reverse=True, finalize=True,
                   emit_state=emit_state, nbat=nbat, tb=tb, yf=yf, za=za)
    y = _out_stage(x, uvz, ya, mod, lw["ln_post"], lw["sgu"], lw["ws_cat"], lw["bs_x"],
                   lw["w_out_bf"], tm_out)
    return y, sf, sb


def _pad_rows(a, n):
    return jnp.concatenate([a, jnp.zeros((n - a.shape[0],) + a.shape[1:], a.dtype)], axis=0)


def _to_compact(s):
    b = s.shape[0]
    return jnp.transpose(s, (0, 2, 1, 3)).reshape(b, HEAD, D_A)


def _from_compact(sc):
    b = sc.shape[0]
    return jnp.transpose(sc.reshape(b, HEAD, H_A, HEAD), (0, 2, 1, 3))


def kernel(x_prompt, x_sample, c, state_fwd, state_bwd, c_ctx, ln_pre, ln_post, w_mod, b_mod, w_in, ts_mu, w0, w_up, a0, a_up, k_k, k_a, r_k, gn_w, gn_b, sgu_ln_g, sgu_ln_b, w_s, b_s, w_out):
    depth = w_in.shape[0]
    batch, seq, _ = x_prompt.shape
    dec_batch, dec_seq, _ = x_sample.shape

    ti = jnp.arange(SCAN_CHUNK)
    tri = [(ti[None, :] <= ti[:, None]).astype(BF16),
           (ti[None, :] >= ti[:, None]).astype(BF16)]
    gi = jnp.arange(GROUP_LANES) // HEAD
    ones_bd = (gi[:, None] == gi[None, :]).astype(BF16)

    cvecs = _pad_rows(jnp.concatenate([c_ctx[None, :], c], axis=0), 16)

    y_ctx = x_prompt
    y_lat = x_sample
    new_f, new_b = [], []
    for l in range(depth):
        mods = _modulation(cvecs, w_mod[l], b_mod[l][None, :]).reshape(16, 3, D_MODEL)
        mod_ctx = mods[0:1]
        mod_lat = mods[1:1 + dec_batch]

        zpad = jnp.zeros((LORA, D_A), F32)
        wup = [jnp.concatenate([w_up[l, 0], zpad], 0).astype(BF16),
               jnp.concatenate([zpad, w_up[l, 1]], 0).astype(BF16)]
        aup = [jnp.concatenate([a_up[l, 0], zpad], 0).astype(BF16),
               jnp.concatenate([zpad, a_up[l, 1]], 0).astype(BF16)]
        pv = [_pad_rows(jnp.stack([w0[l, d], a0[l, d], k_k[l, d], k_a[l, d]]), 8) for d in range(2)]
        pv2 = _pad_rows(jnp.stack([a0[l, 0], k_a[l, 0], r_k[l, 0].reshape(D_A), r_k[l, 1].reshape(D_A),
                                   gn_w[l], gn_b[l]]), 8)
        consts = {"pv": pv, "pv2": pv2, "wup": wup, "aup": aup, "tri": tri, "ones_bd": ones_bd}
        lw = {
            "ln_pre": ln_pre[l][None, :], "ln_post": ln_post[l][None, :], "ts_mu": ts_mu[l],
            "w_in_bf": w_in[l].astype(BF16), "w_out_bf": w_out[l].astype(BF16),
            "sgu": jnp.stack([sgu_ln_g[l], sgu_ln_b[l]]),
            "ws_cat": jnp.concatenate([w_s[l, 0::2], w_s[l, 1::2]], axis=2).astype(BF16),
            "bs_x": jnp.repeat(b_s[l].T, D_B // H_B, axis=1),
        }

        y_ctx, sf, sb = _layer(y_ctx, jnp.broadcast_to(mod_ctx, (batch, 3, D_MODEL)), None, None, lw,
                               consts, emit_state=True, nbat=SCAN_BATCH, tb=SCAN_BLOCK, tm_in=seq, tm_out=seq)
        new_f.append(_from_compact(sf))
        new_b.append(_from_compact(sb))

        y_lat, _, _ = _layer(y_lat, mod_lat, _to_compact(state_fwd[:, l]), _to_compact(state_bwd[:, l]),
                             lw, consts, emit_state=False, nbat=SCAN_BATCH, tb=SCAN_BLOCK,
                             tm_in=TOKEN_TILE, tm_out=TOKEN_TILE)

    y_prompt = y_ctx
    new_state_fwd = jnp.stack(new_f, axis=1).astype(x_prompt.dtype)
    new_state_bwd = jnp.stack(new_b, axis=1).astype(x_prompt.dtype)
    return (y_prompt, y_lat, new_state_fwd, new_state_bwd)
```

```python
import functools
import math

import jax
import jax.numpy as jnp
from jax import lax
from jax.experimental import pallas as pl
from jax.experimental.pallas import tpu as pltpu

F32 = jnp.float32
BF16 = jnp.bfloat16

D_MODEL = 1024
D_A = 512
D_B = 512
HEAD = 64
H_A = D_A // HEAD
H_B = 8
LORA = 64
GMLP_CHUNK = 128
D_RKV = 3 * D_A
D_LORA = 4 * LORA
D_UVZ = 3 * D_B
D_IN = D_RKV + D_A + D_LORA + D_UVZ
NORM_EPS = 1e-6
GN_EPS = 6.4e-4
L2_EPS = 1e-12
DECAY_SCALE = math.exp(-0.5)

SCAN_CHUNK = 64
GROUP_LANES = 256
HEADS_PER_GROUP = GROUP_LANES // HEAD
N_GROUPS = D_A // GROUP_LANES
HALO = 8
SUBLANES = 8
WAVE_CHUNKS = 4
SCAN_BATCH = 2
SCAN_BLOCK = 256
TOKEN_TILE = 512
OUT_TILE = 1024

VMEM_LIMIT = 56 * 1024 * 1024


def _dot(a, b):
    return jnp.dot(a, b, preferred_element_type=F32)


def _split2(x):
    hi = x.astype(BF16)
    lo = (x - hi.astype(F32)).astype(BF16)
    return hi, lo


def _sigmoid(x):
    return 1.0 / (1.0 + jnp.exp(-x))


def _mod_kernel(c_ref, w_ref, b_ref, o_ref):
    c = c_ref[...]
    s = c * _sigmoid(c)
    sh, sl = _split2(s)
    wh, wl = _split2(w_ref[...])
    o_ref[...] = _dot(sh, wh) + _dot(sh, wl) + _dot(sl, wh) + b_ref[...]


def _modulation(cvecs, w_mod, b_mod):
    n = cvecs.shape[0]
    nblk = 3
    return pl.pallas_call(
        _mod_kernel,
        grid=(nblk,),
        in_specs=[
            pl.BlockSpec((n, D_MODEL), lambda j: (0, 0)),
            pl.BlockSpec((D_MODEL, D_MODEL), lambda j: (0, j)),
            pl.BlockSpec((1, D_MODEL), lambda j: (0, j)),
        ],
        out_specs=pl.BlockSpec((n, D_MODEL), lambda j: (0, j)),
        out_shape=jax.ShapeDtypeStruct((n, 3 * D_MODEL), F32),
        compiler_params=pltpu.CompilerParams(vmem_limit_bytes=VMEM_LIMIT),
        name="modulation",
    )(cvecs, w_mod, b_mod)


def _inproj_kernel(x_ref, xp_ref, xn_ref, mod_ref, lnpre_ref, mu_ref, w_ref,
                   rkv_ref, za_ref, lora_ref, uvz_ref, *, tm, seq_len):
    i = pl.program_id(1)
    x = jnp.concatenate([xp_ref[0], x_ref[0], xn_ref[0]], axis=0)
    shift = mod_ref[0, 0:1, :]
    scale = mod_ref[0, 1:2, :]
    gain = lnpre_ref[...] * (1.0 + scale)
    ms = jnp.mean(x * x, axis=-1, keepdims=True)
    h = (x * lax.rsqrt(ms + NORM_EPS)) * gain + shift
    hb_all = h.astype(BF16)
    hb = hb_all[HALO:HALO + tm]

    o = D_RKV
    za_ref[0] = _dot(hb, w_ref[:, o:o + D_A]); o += D_A
    lora_ref[0] = _dot(hb, w_ref[:, o:o + D_LORA]); o += D_LORA
    uvz_ref[0] = _dot(hb, w_ref[:, o:o + D_UVZ]).astype(uvz_ref.dtype)

    rkv = _dot(hb_all, w_ref[:, 0:D_RKV])
    row = lax.broadcasted_iota(jnp.int32, (tm, D_RKV), 0)
    if seq_len >= tm:
        starts = jnp.logical_and(row == 0, lax.rem(i * tm, seq_len) == 0)
        ends = jnp.logical_and(row == tm - 1, lax.rem((i + 1) * tm, seq_len) == 0)
    else:
        starts = functools.reduce(jnp.logical_or, [row == r for r in range(0, tm, seq_len)])
        ends = functools.reduce(jnp.logical_or, [row == r + seq_len - 1 for r in range(0, tm, seq_len)])
    cur = rkv[HALO:HALO + tm]
    prv = rkv[HALO - 1:HALO - 1 + tm]
    nxt = rkv[HALO + 1:HALO + 1 + tm]
    prv = jnp.where(starts, 0.0, prv)
    nxt = jnp.where(ends, 0.0, nxt)
    mu0 = mu_ref[0:1, :]
    mu1 = mu_ref[1:2, :]
    rkv_ref[0] = (1.0 - mu0 - mu1) * cur + mu0 * prv + mu1 * nxt


def _inproj(x, mod, ln_pre, ts_mu, w_in_bf, tm, seq_len):
    nb, L, _ = x.shape
    hb = tm // HALO
    nh = L // HALO
    tok = lambda d: pl.BlockSpec((1, tm, d), lambda b, i: (b, i, 0))
    shp = lambda d, dt=F32: jax.ShapeDtypeStruct((nb, L, d), dt)
    return pl.pallas_call(
        functools.partial(_inproj_kernel, tm=tm, seq_len=seq_len),
        grid=(nb, L // tm),
        in_specs=[
            tok(D_MODEL),
            pl.BlockSpec((1, HALO, D_MODEL), lambda b, i: (b, jnp.maximum(i * hb - 1, 0), 0)),
            pl.BlockSpec((1, HALO, D_MODEL), lambda b, i: (b, jnp.minimum((i + 1) * hb, nh - 1), 0)),
            pl.BlockSpec((1, 3, D_MODEL), lambda b, i: (b, 0, 0)),
            pl.BlockSpec((1, D_MODEL), lambda b, i: (0, 0)),
            pl.BlockSpec((2, D_RKV), lambda b, i: (0, 0)),
            pl.BlockSpec((D_MODEL, D_IN), lambda b, i: (0, 0)),
        ],
        out_specs=[tok(D_RKV), tok(D_A), tok(D_LORA), tok(D_UVZ)],
        out_shape=[shp(D_RKV), shp(D_A), shp(D_LORA), shp(D_UVZ, BF16)],
        compiler_params=pltpu.CompilerParams(
            dimension_semantics=("arbitrary", "arbitrary"), vmem_limit_bytes=VMEM_LIMIT),
        name="inproj",
    )(x, x, x, mod, ln_pre, ts_mu, w_in_bf)


def _block_diag(x, low):
    z = jnp.zeros((x.shape[0], 128), x.dtype)
    blocks = []
    for h in range(HEADS_PER_GROUP):
        tile = x[:, (h // 2) * 128:(h // 2 + 1) * 128]
        kept = jnp.where(low, tile, 0) if h % 2 == 0 else jnp.where(low, 0, tile)
        blocks.append(jnp.concatenate([kept, z] if h < 2 else [z, kept], axis=1))
    return jnp.concatenate(blocks, axis=0)


def _block_diag_t(xt, low):
    z = jnp.zeros((HEAD, 128), xt.dtype)
    blocks = []
    for h in range(HEADS_PER_GROUP):
        tile = xt[h * HEAD:(h + 1) * HEAD, :]
        kept = jnp.where(low, tile, 0) if h % 2 == 0 else jnp.where(low, 0, tile)
        blocks.append(jnp.concatenate([kept, z] if h < 2 else [z, kept], axis=1))
    return jnp.concatenate(blocks, axis=0)


def _head_sum(x, ones_bd):
    rows = x.shape[0]
    xb = x.astype(BF16)
    parts = [xb[:, g * GROUP_LANES:(g + 1) * GROUP_LANES] for g in range(N_GROUPS)]
    s = _dot(jnp.concatenate(parts, axis=0), ones_bd)
    return jnp.concatenate([s[g * rows:(g + 1) * rows] for g in range(N_GROUPS)], axis=1)


def _interleave(threads):
    gens = [g for g, _ in threads]
    total = [n for _, n in threads]
    done = [0] * len(gens)
    alive = [True] * len(gens)
    while any(alive):
        k = min((i for i in range(len(gens)) if alive[i]), key=lambda i: done[i] / total[i])
        try:
            next(gens[k])
            done[k] += 1
        except StopIteration:
            alive[k] = False


def _scan_kernel(*refs, reverse, finalize, zero_init, emit_state, nbat, tb, nj, n_items):
    it = iter(refs)
    rkv_ref, lora_ref = next(it), next(it)
    s0_ref = None if zero_init else next(it)
    pv_ref, wup_ref, aup_ref = next(it), next(it), next(it)
    tri_ref, ones_ref = next(it), next(it)
    if finalize:
        yf_ref, za_ref, pv2_ref, aupf_ref = next(it), next(it), next(it), next(it)
    y_ref = next(it)
    sout_ref = next(it) if emit_state else None
    s_scr = next(it)
    at_s, rt_s, vb_s, gT_s = (next(it) for _ in range(4))
    btT_s, ktT_s = next(it), next(it)
    bon_s = next(it) if finalize else None
    tt_s, arb_s, avk_s, y0_s = (next(it) for _ in range(4))
    yd_ref = next(it) if finalize else y_ref

    C = SCAN_CHUNK
    G = GROUP_LANES
    n_chunks = tb // C

    i = pl.program_id(0)
    w0_slot = lax.rem(i, 3)
    r1_slot = lax.rem(i + 2, 3)
    r2_slot = lax.rem(i + 1, 3)
    w1_slot = lax.rem(i + 1, 2)
    q2_slot = lax.rem(i, 2)
    item2 =jnp.clip(i - 2, 0, n_items - 1)
    jpos2 = lax.rem(item2, nj)
    first2 = jpos2 == 0
    last2 = jnp.logical_and(jpos2 == nj - 1, i >= 2)

    @pl.when(i == 0)
    def _():
        for ref in (at_s, rt_s, vb_s, gT_s, btT_s, ktT_s, tt_s, arb_s, avk_s, y0_s) + (
                (bon_s,) if finalize else ()):
            ref[...] = jnp.zeros_like(ref)
        s_scr[...] = jnp.zeros_like(s_scr)

    row = lax.broadcasted_iota(jnp.int32, (C, G), 0)
    col = lax.broadcasted_iota(jnp.int32, (C, G), 1)
    scol = col % HEAD
    hcol = col // HEAD
    if reverse:
        m_strict, m_incl = scol > row, scol >= row
    else:
        m_strict, m_incl = scol < row, scol <= row
    eye = jnp.where(scol == row, 1.0, 0.0).astype(F32)
    last_row = 0 if reverse else C - 1
    low = lax.broadcasted_iota(jnp.int32, (C, 128), 1) < HEAD
    low_t = lax.broadcasted_iota(jnp.int32, (G, 128), 1) < HEAD

    w0, a0, k_k, k_a = pv_ref[0:1, :], pv_ref[1:2, :], pv_ref[2:3, :], pv_ref[3:4, :]

    def prep_stream():
        for bi in range(nbat):
            for c in range(n_chunks):
                rows = slice(c * C, (c + 1) * C)
                r = rkv_ref[bi, rows, 0:D_A]
                k = rkv_ref[bi, rows, D_A:2 * D_A]
                v = rkv_ref[bi, rows, 2 * D_A:3 * D_A]
                vb_s[w0_slot, bi, rows, :] = v.astype(BF16)
                lo = lora_ref[bi, rows, :]
                wd = jnp.tanh(lo[:, 0:2 * LORA]).astype(BF16)
                ad = lo[:, 2 * LORA:4 * LORA].astype(BF16)
                ld = -DECAY_SCALE * _sigmoid(w0 + _dot(wd, wup_ref[...]))
                a = _sigmoid(a0 + _dot(ad, aup_ref[...]))
                kk = k * k_k
                kk2 = _head_sum(kk * kk, ones_ref[...])
                hi, lo2 = _split2(ld)
                cum = _dot(tri_ref[...], jnp.concatenate([hi, lo2], axis=1))
                yield

                cum = cum[:, 0:D_A] + cum[:, D_A:2 * D_A]
                kk = kk * lax.rsqrt(kk2 + L2_EPS)
                k_d = k * (1.0 + (a - 1.0) * k_a)
                tot = cum[last_row:last_row + 1, :]
                et = jnp.broadcast_to(jnp.exp(tot), (2 * C, D_A)).T
                gT_s[w0_slot, bi, c] = jnp.concatenate(
                    [jnp.where(low, et[(2 * p) * HEAD:(2 * p + 1) * HEAD, :], et[(2 * p + 1) * HEAD:(2 * p + 2) * HEAD, :])
                     for p in range(H_A // 2)], axis=1)
                e_neg = jnp.exp(-cum)
                at_s[w0_slot, bi, rows, :] = (-kk * jnp.exp(cum - ld)).astype(BF16)
                rt_s[w0_slot, bi, rows, :] = (r * jnp.exp(cum)).astype(BF16)
                bt = kk * a * e_neg
                kt = k_d * e_neg
                btT_s[w0_slot, bi, c] = jnp.concatenate([bt, bt], axis=0).T.astype(BF16)
                ktT_s[w0_slot, bi, c] = jnp.concatenate([kt, kt], axis=0).T.astype(BF16)
                yield

                if finalize:
                    a0f, k_af = pv2_ref[0:1, :], pv2_ref[1:2, :]
                    r_kf, r_kb = pv2_ref[2:3, :], pv2_ref[3:4, :]
                    a_f = _sigmoid(a0f + _dot(ad, aupf_ref[...]))
                    k_df = k * (1.0 + (a_f - 1.0) * k_af)
                    bon_s[w0_slot, bi, rows, :] = _head_sum(r * (k_df * r_kf + k_d * r_kb), ones_ref[...]) * v
                    yield

    prep_ticks = nbat * n_chunks * (3 if finalize else 2)

    n_sq = int(math.log2(C)) - 2
    waves = [(bi, c0) for bi in range(nbat) for c0 in range(0, n_chunks, WAVE_CHUNKS)]

    def matrix_stream():
        for bi, c0 in waves:
            chains = [(c, g) for c in range(c0, c0 + WAVE_CHUNKS) for g in range(N_GROUPS)]

            def ld_(ref, c, g):
                return ref[r1_slot, bi, c * C:(c + 1) * C, g * G:(g + 1) * G]

            def st_(ref, c, g, val):
                ref[w1_slot, bi, c * C:(c + 1) * C, g * G:(g + 1) * G] = val

            ar = [jnp.concatenate([ld_(at_s, c, g), ld_(rt_s, c, g)], axis=0) for c, g in chains]
            a_b = [_dot(ar[n], _block_diag_t(btT_s[r1_slot, bi, c, g * G:(g + 1) * G, :], low))
                   for n, (c, g) in enumerate(chains)]
            a_k = [_dot(ar[n], _block_diag_t(ktT_s[r1_slot, bi, c, g * G:(g + 1) * G, :], low))
                   for n, (c, g) in enumerate(chains)]
            yield
            a_ab = [jnp.where(m_strict, x[0:C], 0.0) for x in a_b]
            for n, (c, g) in enumerate(chains):
                st_(arb_s, c, g, jnp.where(m_incl, a_b[n][C:2 * C], 0.0).astype(BF16))
            a_kk = [jnp.concatenate([jnp.where(m_strict, x[0:C], 0.0), jnp.where(m_incl, x[C:2 * C], 0.0)],
                                    axis=0).astype(BF16) for x in a_k]
            pb = [x.astype(BF16) for x in a_ab]
            pb = [_dot(x, _block_diag(x, low)).astype(BF16) for x in pb]
            av = [_dot(a_kk[n], _block_diag(ld_(vb_s, c, g), low)) for n, (c, g) in enumerate(chains)]
            t = [eye + x for x in a_ab]
            yield
            for n, (c, g) in enumerate(chains):
                st_(avk_s, c, g, av[n][0:C])
                st_(y0_s, c, g, av[n][C:2 * C])
            for _ in range(n_sq):
                pt = [_dot(jnp.concatenate([pb[n], t[n].astype(BF16)], axis=0), _block_diag(pb[n], low))
                      for n in range(len(chains))]
                pb = [x[0:C].astype(BF16) for x in pt]
                t = [t[n] + pt[n][C:2 * C] for n in range(len(chains))]
                yield
            for n, (c, g) in enumerate(chains):
                tf = t[n] + _dot(t[n].astype(BF16), _block_diag(pb[n], low))
                st_(tt_s, c, g, tf.astype(BF16))
            yield

    matrix_ticks = len(waves) * (n_sq + 3)

    def state_stream(bi):
        if zero_init:
            s_scr[bi] = jnp.where(first2, 0.0, s_scr[bi])
        else:
            s_scr[bi] = jnp.where(first2, s0_ref[bi], s_scr[bi])
        yield
        sls = [slice(g * G, (g + 1) * G) for g in range(N_GROUPS)]
        for ci in range(n_chunks):
            c = (n_chunks - 1 - ci) if reverse else ci
            rows = slice(c * C, (c + 1) * C)
            s_c = [s_scr[bi, :, sl] for sl in sls]
            ar = [jnp.concatenate([at_s[r2_slot, bi, rows, sl], rt_s[r2_slot, bi, rows, sl]], axis=0) for sl in sls]
            xs_ = [_dot(ar[g], _block_diag(s_c[g].astype(BF16), low)) for g in range(N_GROUPS)]
            yield
            w = [xs_[g][0:C] + avk_s[q2_slot, bi, rows, sl] for g, sl in enumerate(sls)]
            u = [_dot(tt_s[q2_slot, bi, rows, sl], _block_diag(w[g].astype(BF16), low)) for g, sl in enumerate(sls)]
            yield
            ub = [x.astype(BF16) for x in u]
            full = [_dot(jnp.where(low_t, btT_s[r2_slot, bi, c, sl, :], ktT_s[r2_slot, bi, c, sl, :]),
                         jnp.concatenate([ub[g], vb_s[r2_slot, bi, rows, sl]], axis=0))
                    for g, sl in enumerate(sls)]
            yv = [_dot(arb_s[q2_slot, bi, rows, sl], _block_diag(ub[g], low)) for g, sl in enumerate(sls)]
            for g, sl in enumerate(sls):
                upd = jnp.zeros((HEAD, G), F32)
                for h in range(HEADS_PER_GROUP):
                    upd = jnp.where(hcol == h, full[g][h * HEAD:(h + 1) * HEAD, :], upd)
                s_scr[bi, :, sl] = (s_c[g] + upd) * gT_s[r2_slot, bi, c, :, sl]
                yd_ref[bi, rows, sl] = xs_[g][C:2 * C] + yv[g] + y0_s[q2_slot, bi, rows, sl]
            yield
        if finalize:
            gn_w, gn_b = pv2_ref[4:5, :], pv2_ref[5:6, :]
            for hlf in range(2):
                rows = slice(hlf * (tb // 2), (hlf + 1) * (tb // 2))
                yt = yf_ref[bi, rows, :] + yd_ref[bi, rows, :]
                mean = _head_sum(yt, ones_ref[...]) * (1.0 / HEAD)
                yc = yt - mean
                var = _head_sum(yc * yc, ones_ref[...]) * (1.0 / HEAD)
                yn = yc * lax.rsqrt(var + GN_EPS) * gn_w + gn_b + bon_s[r2_slot, bi, rows, :]
                z = za_ref[bi, rows, :]
                y_ref[bi, rows, :] = (yn * (z * _sigmoid(z))).astype(y_ref.dtype)
                yield

    state_ticks = 1 + 3 * n_chunks + (2 if finalize else 0)

    _interleave([(prep_stream(), prep_ticks), (matrix_stream(), matrix_ticks)]
                + [(state_stream(bi), state_ticks) for bi in range(nbat)])

    if emit_state:
        @pl.when(last2)
        def _():
            sout_ref[...] = s_scr[...]


def _scan(rkv, lora, s0c, consts, *, reverse, finalize, emit_state, nbat, tb, yf=None, za=None):
    nb, L, _ = rkv.shape
    nj = L // tb
    n_items = (nb // nbat) * nj
    zero_init = s0c is None
    d = 1 if reverse else 0

    def pos(item):
        jpos = lax.rem(item, nj)
        return item // nj, ((nj - 1 - jpos) if reverse else jpos)

    def item0(i):
        return jnp.minimum(i, n_items - 1)

    def item2(i):
        return jnp.clip(i - 2, 0, n_items - 1)

    def tok(w, item_of):
        def imap(i):
            p, jj = pos(item_of(i))
            return (p, jj, 0)
        return pl.BlockSpec((nbat, tb, w), imap)

    full2 = lambda a: pl.BlockSpec(a.shape, lambda i: (0, 0))
    st_spec = pl.BlockSpec((nbat, HEAD, D_A), lambda i: (pos(item2(i))[0], 0, 0))

    args = [rkv, lora]
    in_specs = [tok(D_RKV, item0), tok(D_LORA, item0)]
    if not zero_init:
        args.append(s0c)
        in_specs.append(st_spec)
    small = [consts["pv"][d], consts["wup"][d], consts["aup"][d], consts["tri"][d], consts["ones_bd"]]
    args += small
    in_specs += [full2(a) for a in small]
    if finalize:
        args += [yf, za, consts["pv2"], consts["aup"][0]]
        in_specs += [tok(D_A, item2), tok(D_A, item2), full2(consts["pv2"]), full2(consts["aup"][0])]

    out_shape = [jax.ShapeDtypeStruct((nb, L, D_A), BF16 if finalize else F32)]
    out_specs = [tok(D_A, item2)]
    if emit_state:
        out_shape.append(jax.ShapeDtypeStruct((nb, HEAD, D_A), F32))
        out_specs.append(st_spec)

    tokbuf = lambda slots, dt: pltpu.VMEM((slots, nbat, tb, D_A), dt)
    scratch = [pltpu.VMEM((nbat, HEAD, D_A), F32)]
    scratch += [tokbuf(3, BF16)] * 3
    scratch += [pltpu.VMEM((3, nbat, tb // SCAN_CHUNK, HEAD, D_A), F32)]
    scratch += [pltpu.VMEM((3, nbat, tb // SCAN_CHUNK, D_A, 2 * SCAN_CHUNK), BF16)] * 2
    if finalize:
        scratch += [tokbuf(3, F32)]
    scratch += [tokbuf(2, BF16)] * 2 + [tokbuf(2, F32)] * 2
    if finalize:
        scratch += [pltpu.VMEM((nbat, tb, D_A), F32)]

    kern = functools.partial(_scan_kernel, reverse=reverse, finalize=finalize, zero_init=zero_init,
                             emit_state=emit_state, nbat=nbat, tb=tb, nj=nj, n_items=n_items)
    outs = pl.pallas_call(
        kern,
        grid=(n_items + 2,),
        in_specs=in_specs,
        out_specs=out_specs,
        out_shape=out_shape,
        scratch_shapes=scratch,
        compiler_params=pltpu.CompilerParams(
            dimension_semantics=("arbitrary",), vmem_limit_bytes=VMEM_LIMIT),
        name="scan_bwd" if reverse else "scan_fwd",
    )(*args)
    return outs if emit_state else (outs[0], None)


def _out_kernel(x_ref, uvz_ref, ya_ref, mod_ref, lnpost_ref, sgu_ref, ws_ref, bs_ref, wout_ref, o_ref, *, tm):
    gate = mod_ref[0, 2:3, :]
    ln_g = sgu_ref[0:1, :]
    ln_b = sgu_ref[1:2, :]
    lane = lax.broadcasted_iota(jnp.int32, (GMLP_CHUNK, 128), 1)
    low = lane < (D_B // H_B)
    post_gain = lnpost_ref[...] * gate

    def chunk_stream(q):
        rows = slice(q * GMLP_CHUNK, (q + 1) * GMLP_CHUNK)
        vb = uvz_ref[0, rows, D_B:2 * D_B].astype(F32)
        mu = jnp.mean(vb, axis=-1, keepdims=True)
        vc = vb - mu
        var = jnp.mean(vc * vc, axis=-1, keepdims=True)
        vn = (vc * lax.rsqrt(var + NORM_EPS)) * ln_g + ln_b
        yield
        parts = []
        for pr in range(H_B // 2):
            vp = vn[:, pr * 128:(pr + 1) * 128]
            rhs = jnp.concatenate([jnp.where(low, vp, 0.0), jnp.where(low, 0.0, vp)], axis=0)
            parts.append(_dot(ws_ref[pr], rhs.astype(BF16)))
        yield
        u = uvz_ref[0, rows, 0:D_B].astype(F32)
        zb = uvz_ref[0, rows, 2 * D_B:3 * D_B].astype(F32)
        s = jnp.concatenate(parts, axis=1) + bs_ref[...]
        yb = u * s * (zb * _sigmoid(zb))
        mixed = jnp.concatenate([ya_ref[0, rows, :], yb.astype(BF16)], axis=1)
        out = _dot(mixed, wout_ref[...])
        yield
        ms = jnp.mean(out * out, axis=-1, keepdims=True)
        o_ref[0, rows, :] = x_ref[0, rows, :] + (out * lax.rsqrt(ms + NORM_EPS)) * post_gain
        yield

    n_q = tm // GMLP_CHUNK
    streams = [chunk_stream(q) for q in range(n_q)]
    for step in range(n_q + 3):
        for q in range(n_q):
            if 0 <= step - q < 4:
                next(streams[q])


def _out_stage(x, uvz, ya, mod, ln_post, sgu, ws_cat, bs_x, w_out_bf, tm):
    nb, L, _ = x.shape
    tok = lambda d: pl.BlockSpec((1, tm, d), lambda b, i: (b, i, 0))
    full = lambda a: pl.BlockSpec(a.shape, lambda b, i: (0,) * a.ndim)
    return pl.pallas_call(
        functools.partial(_out_kernel, tm=tm),
        grid=(nb, L // tm),
        in_specs=[tok(D_MODEL), tok(D_UVZ), tok(D_A),
                  pl.BlockSpec((1, 3, D_MODEL), lambda b, i: (b, 0, 0)),
                  full(ln_post), full(sgu), full(ws_cat), full(bs_x), full(w_out_bf)],
        out_specs=tok(D_MODEL),
        out_shape=jax.ShapeDtypeStruct((nb, L, D_MODEL), F32),
        compiler_params=pltpu.CompilerParams(
            dimension_semantics=("arbitrary", "arbitrary"), vmem_limit_bytes=VMEM_LIMIT),
        name="out_stage",
    )(x, uvz, ya, mod, ln_post, sgu, ws_cat, bs_x, w_out_bf)


def _layer(x, mod, s0f, s0b, lw, consts, *, n_seq, seq_len, emit_state):
    nb, ntok, _ = x.shape
    per_seq = lambda a: a.reshape(n_seq, seq_len, a.shape[-1])
    rkv, za, lora, uvz = _inproj(x, mod, lw["ln_pre"], lw["ts_mu"], lw["w_in_bf"], TOKEN_TILE, seq_len)
    rkv, za, lora = per_seq(rkv), per_seq(za), per_seq(lora)
    yf, sf = _scan(rkv, lora, s0f, consts, reverse=False, finalize=False,
                   emit_state=emit_state, nbat=SCAN_BATCH, tb=SCAN_BLOCK)
    ya, sb = _scan(rkv, lora, s0b, consts, reverse=True, finalize=True,
                   emit_state=emit_state, nbat=SCAN_BATCH, tb=SCAN_BLOCK, yf=yf, za=za)
    y = _out_stage(x, uvz, ya.reshape(nb, ntok, D_A), mod, lw["ln_post"], lw["sgu"], lw["ws_cat"], lw["bs_x"],
                   lw["w_out_bf"], OUT_TILE)
    return y, sf, sb


def _pad_rows(a, n):
    return jnp.concatenate([a, jnp.zeros((n - a.shape[0],) + a.shape[1:], a.dtype)], axis=0)


def _to_compact(s):
    b = s.shape[0]
    return jnp.transpose(s, (0, 3, 1, 2)).reshape(b, HEAD, D_A)


def _from_compact(sc):
    b = sc.shape[0]
    return jnp.transpose(sc.reshape(b, HEAD, H_A, HEAD), (0, 2, 3, 1))


def kernel(x_prompt, x_sample, c, state_fwd, state_bwd, c_ctx, ln_pre, ln_post, w_mod, b_mod, w_in, ts_mu, w0, w_up, a0, a_up, k_k, k_a, r_k, gn_w, gn_b, sgu_ln_g, sgu_ln_b, w_s, b_s, w_out):
    depth = w_in.shape[0]
    batch, seq, _ = x_prompt.shape
    dec_batch, dec_seq, _ = x_sample.shape

    ti = jnp.arange(SCAN_CHUNK)
    tri = [(ti[None, :] <= ti[:, None]).astype(BF16),
           (ti[None, :] >= ti[:, None]).astype(BF16)]
    gi = jnp.arange(GROUP_LANES) // HEAD
    ones_bd = (gi[:, None] == gi[None, :]).astype(BF16)

    cvecs = _pad_rows(jnp.concatenate([c_ctx[None, :], c], axis=0), 16)

    y_ctx = x_prompt.reshape(1, batch * seq, D_MODEL)
    y_lat = x_sample
    new_f, new_b = [], []
    for l in range(depth):
        mods = _modulation(cvecs, w_mod[l], b_mod[l][None, :]).reshape(16, 3, D_MODEL)
        mod_ctx = mods[0:1]
        mod_lat = mods[1:1 + dec_batch]

        zpad = jnp.zeros((LORA, D_A), F32)
        wup = [jnp.concatenate([w_up[l, 0], zpad], 0).astype(BF16),
               jnp.concatenate([zpad, w_up[l, 1]], 0).astype(BF16)]
        aup = [jnp.concatenate([a_up[l, 0], zpad], 0).astype(BF16),
               jnp.concatenate([zpad, a_up[l, 1]], 0).astype(BF16)]
        pv = [_pad_rows(jnp.stack([w0[l, d], a0[l, d], k_k[l, d], k_a[l, d]]), 8) for d in range(2)]
        pv2 = _pad_rows(jnp.stack([a0[l, 0], k_a[l, 0], r_k[l, 0].reshape(D_A), r_k[l, 1].reshape(D_A),
                                   gn_w[l], gn_b[l]]), 8)
        consts = {"pv": pv, "pv2": pv2, "wup": wup, "aup": aup, "tri": tri, "ones_bd": ones_bd}
        lw = {
            "ln_pre": ln_pre[l][None, :], "ln_post": ln_post[l][None, :], "ts_mu": ts_mu[l],
            "w_in_bf": w_in[l].astype(BF16), "w_out_bf": w_out[l].astype(BF16),
            "sgu": jnp.stack([sgu_ln_g[l], sgu_ln_b[l]]),
            "ws_cat": jnp.concatenate([w_s[l, 0::2], w_s[l, 1::2]], axis=2).astype(BF16),
            "bs_x": jnp.repeat(b_s[l].T, D_B // H_B, axis=1),
        }

        y_ctx, sf, sb = _layer(y_ctx, mod_ctx, None, None, lw, consts,
                               n_seq=batch, seq_len=seq, emit_state=True)
        new_f.append(_from_compact(sf))
        new_b.append(_from_compact(sb))

        y_lat, _, _ = _layer(y_lat, mod_lat, _to_compact(state_fwd[:, l]), _to_compact(state_bwd[:, l]),
                             lw, consts, n_seq=dec_batch, seq_len=dec_seq, emit_state=False)

    y_prompt = y_ctx.reshape(batch, seq, D_MODEL)
    new_state_fwd = jnp.stack(new_f, axis=1).astype(x_prompt.dtype)
    new_state_bwd = jnp.stack(new_b, axis=1).astype(x_prompt.dtype)
    return (y_prompt, y_lat, new_state_fwd, new_state_bwd)
```

```python
import functools
import math

import jax
import jax.numpy as jnp
from jax import lax
from jax.experimental import pallas as pl
from jax.experimental.pallas import tpu as pltpu

F32 = jnp.float32
BF16 = jnp.bfloat16

D_MODEL = 1024
D_A = 512
D_B = 512
HEAD = 64
H_A = D_A // HEAD
H_B = 8
LORA = 64
GMLP_CHUNK = 128
D_RKV = 3 * D_A
D_LORA = 4 * LORA
D_UVZ = 3 * D_B
D_IN = D_RKV + D_A + D_LORA + D_UVZ
NORM_EPS = 1e-6
GN_EPS = 6.4e-4
L2_EPS = 1e-12
DECAY_SCALE = math.exp(-0.5)

SCAN_CHUNK = 64
GROUP_LANES = 256
HEADS_PER_GROUP = GROUP_LANES // HEAD
N_GROUPS = D_A // GROUP_LANES
HALO = 8
SUBLANES = 8
WAVE_CHUNKS = 4
SCAN_BATCH = 2
SCAN_BLOCK = 256
TOKEN_TILE = 1024
OUT_TILE = 1024

VMEM_LIMIT = 56 * 1024 * 1024


def _dot(a, b):
    return jnp.dot(a, b, preferred_element_type=F32)


def _split2(x):
    hi = x.astype(BF16)
    lo = (x - hi.astype(F32)).astype(BF16)
    return hi, lo


def _sigmoid(x):
    return 1.0 / (1.0 + jnp.exp(-x))


def _mod_kernel(c_ref, w_ref, b_ref, o_ref):
    c = c_ref[...]
    s = c * _sigmoid(c)
    sh, sl = _split2(s)
    wh, wl = _split2(w_ref[...])
    o_ref[...] = _dot(sh, wh) + _dot(sh, wl) + _dot(sl, wh) + b_ref[...]


def _modulation(cvecs, w_mod, b_mod):
    n = cvecs.shape[0]
    nblk = 3
    return pl.pallas_call(
        _mod_kernel,
        grid=(nblk,),
        in_specs=[
            pl.BlockSpec((n, D_MODEL), lambda j: (0, 0)),
            pl.BlockSpec((D_MODEL, D_MODEL), lambda j: (0, j)),
            pl.BlockSpec((1, D_MODEL), lambda j: (0, j)),
        ],
        out_specs=pl.BlockSpec((n, D_MODEL), lambda j: (0, j)),
        out_shape=jax.ShapeDtypeStruct((n, 3 * D_MODEL), F32),
        compiler_params=pltpu.CompilerParams(vmem_limit_bytes=VMEM_LIMIT),
        name="modulation",
    )(cvecs, w_mod, b_mod)


def _inproj_kernel(x_ref, xp_ref, xn_ref, mod_ref, lnpre_ref, mu_ref, w_ref,
                   rkv_ref, za_ref, lora_ref, uvz_ref, *, tm, seq_len):
    i = pl.program_id(1)
    x = jnp.concatenate([xp_ref[0], x_ref[0], xn_ref[0]], axis=0)
    shift = mod_ref[0, 0:1, :]
    scale = mod_ref[0, 1:2, :]
    gain = lnpre_ref[...] * (1.0 + scale)
    ms = jnp.mean(x * x, axis=-1, keepdims=True)
    h = (x * lax.rsqrt(ms + NORM_EPS)) * gain + shift
    hb_all = h.astype(BF16)
    hb = hb_all[HALO:HALO + tm]

    o = D_RKV
    za_ref[0] = _dot(hb, w_ref[:, o:o + D_A]); o += D_A
    lora_ref[0] = _dot(hb, w_ref[:, o:o + D_LORA]); o += D_LORA
    uvz_ref[0] = _dot(hb, w_ref[:, o:o + D_UVZ]).astype(uvz_ref.dtype)

    rkv = _dot(hb_all, w_ref[:, 0:D_RKV])
    row = lax.broadcasted_iota(jnp.int32, (tm, D_RKV), 0)
    if seq_len >= tm:
        starts = jnp.logical_and(row == 0, lax.rem(i * tm, seq_len) == 0)
        ends = jnp.logical_and(row == tm - 1, lax.rem((i + 1) * tm, seq_len) == 0)
    else:
        starts = functools.reduce(jnp.logical_or, [row == r for r in range(0, tm, seq_len)])
        ends = functools.reduce(jnp.logical_or, [row == r + seq_len - 1 for r in range(0, tm, seq_len)])
    cur = rkv[HALO:HALO + tm]
    prv = rkv[HALO - 1:HALO - 1 + tm]
    nxt = rkv[HALO + 1:HALO + 1 + tm]
    prv = jnp.where(starts, 0.0, prv)
    nxt = jnp.where(ends, 0.0, nxt)
    mu0 = mu_ref[0:1, :]
    mu1 = mu_ref[1:2, :]
    rkv_ref[0] = (1.0 - mu0 - mu1) * cur + mu0 * prv + mu1 * nxt


def _inproj(x, mod, ln_pre, ts_mu, w_in_bf, tm, seq_len):
    nb, L, _ = x.shape
    hb = tm // HALO
    nh = L // HALO
    tok = lambda d: pl.BlockSpec((1, tm, d), lambda b, i: (b, i, 0))
    shp = lambda d, dt=F32: jax.ShapeDtypeStruct((nb, L, d), dt)
    return pl.pallas_call(
        functools.partial(_inproj_kernel, tm=tm, seq_len=seq_len),
        grid=(nb, L // tm),
        in_specs=[
            tok(D_MODEL),
            pl.BlockSpec((1, HALO, D_MODEL), lambda b, i: (b, jnp.maximum(i * hb - 1, 0), 0)),
            pl.BlockSpec((1, HALO, D_MODEL), lambda b, i: (b, jnp.minimum((i + 1) * hb, nh - 1), 0)),
            pl.BlockSpec((1, 3, D_MODEL), lambda b, i: (b, 0, 0)),
            pl.BlockSpec((1, D_MODEL), lambda b, i: (0, 0)),
            pl.BlockSpec((2, D_RKV), lambda b, i: (0, 0)),
            pl.BlockSpec((D_MODEL, D_IN), lambda b, i: (0, 0)),
        ],
        out_specs=[tok(D_RKV), tok(D_A), tok(D_LORA), tok(D_UVZ)],
        out_shape=[shp(D_RKV), shp(D_A), shp(D_LORA), shp(D_UVZ, BF16)],
        compiler_params=pltpu.CompilerParams(
            dimension_semantics=("arbitrary", "arbitrary"), vmem_limit_bytes=VMEM_LIMIT),
        name="inproj",
    )(x, x, x, mod, ln_pre, ts_mu, w_in_bf)


def _block_diag(x, low):
    z = jnp.zeros((x.shape[0], 128), x.dtype)
    blocks = []
    for h in range(HEADS_PER_GROUP):
        tile = x[:, (h // 2) * 128:(h // 2 + 1) * 128]
        kept = jnp.where(low, tile, 0) if h % 2 == 0 else jnp.where(low, 0, tile)
        blocks.append(jnp.concatenate([kept, z] if h < 2 else [z, kept], axis=1))
    return jnp.concatenate(blocks, axis=0)


def _block_diag_t(xt, low):
    z = jnp.zeros((HEAD, 128), xt.dtype)
    blocks = []
    for h in range(HEADS_PER_GROUP):
        tile = xt[h * HEAD:(h + 1) * HEAD, :]
        kept = jnp.where(low, tile, 0) if h % 2 == 0 else jnp.where(low, 0, tile)
        blocks.append(jnp.concatenate([kept, z] if h < 2 else [z, kept], axis=1))
    return jnp.concatenate(blocks, axis=0)


def _head_sum(x, ones_bd):
    rows = x.shape[0]
    xb = x.astype(BF16)
    parts = [xb[:, g * GROUP_LANES:(g + 1) * GROUP_LANES] for g in range(N_GROUPS)]
    s = _dot(jnp.concatenate(parts, axis=0), ones_bd)
    return jnp.concatenate([s[g * rows:(g + 1) * rows] for g in range(N_GROUPS)], axis=1)


def _interleave(threads):
    gens = [g for g, _ in threads]
    total = [n for _, n in threads]
    done = [0] * len(gens)
    alive = [True] * len(gens)
    while any(alive):
        k = min((i for i in range(len(gens)) if alive[i]), key=lambda i: done[i] / total[i])
        try:
            next(gens[k])
            done[k] += 1
        except StopIteration:
            alive[k] = False


def _scan_kernel(*refs, reverse, finalize, zero_init, emit_state, nbat, tb, nj, n_items):
    it = iter(refs)
    rkv_ref, lora_ref = next(it), next(it)
    s0_ref = None if zero_init else next(it)
    pv_ref, wup_ref, aup_ref = next(it), next(it), next(it)
    tri_ref, ones_ref = next(it), next(it)
    if finalize:
        yf_ref, za_ref, pv2_ref, aupf_ref = next(it), next(it), next(it), next(it)
    y_ref = next(it)
    sout_ref = next(it) if emit_state else None
    s_scr = next(it)
    at_s, rt_s, vb_s, gT_s = (next(it) for _ in range(4))
    btT_s, ktT_s = next(it), next(it)
    bon_s = next(it) if finalize else None
    tt_s, arb_s, avk_s, y0_s = (next(it) for _ in range(4))
    yd_ref = next(it) if finalize else y_ref

    C = SCAN_CHUNK
    G = GROUP_LANES
    n_chunks = tb // C

    i = pl.program_id(0)
    w0_slot = lax.rem(i, 3)
    r1_slot = lax.rem(i + 2, 3)
    r2_slot = lax.rem(i + 1, 3)
    w1_slot = lax.rem(i + 1, 2)
    q2_slot = lax.rem(i, 2)
    item2 =jnp.clip(i - 2, 0, n_items - 1)
    jpos2 = lax.rem(item2, nj)
    first2 = jpos2 == 0
    last2 = jnp.logical_and(jpos2 == nj - 1, i >= 2)

    @pl.when(i == 0)
    def _():
        for ref in (at_s, rt_s, vb_s, gT_s, btT_s, ktT_s, tt_s, arb_s, avk_s, y0_s) + (
                (bon_s,) if finalize else ()):
            ref[...] = jnp.zeros_like(ref)
        s_scr[...] = jnp.zeros_like(s_scr)

    row = lax.broadcasted_iota(jnp.int32, (C, G), 0)
    col = lax.broadcasted_iota(jnp.int32, (C, G), 1)
    scol = col % HEAD
    hcol = col // HEAD
    if reverse:
        m_strict, m_incl = scol > row, scol >= row
    else:
        m_strict, m_incl = scol < row, scol <= row
    eye = jnp.where(scol == row, 1.0, 0.0).astype(F32)
    last_row = 0 if reverse else C - 1
    low = lax.broadcasted_iota(jnp.int32, (C, 128), 1) < HEAD
    low_t = lax.broadcasted_iota(jnp.int32, (G, 128), 1) < HEAD

    w0, a0, k_k, k_a = pv_ref[0:1, :], pv_ref[1:2, :], pv_ref[2:3, :], pv_ref[3:4, :]

    def prep_stream():
        for bi in range(nbat):
            for c in range(n_chunks):
                rows = slice(c * C, (c + 1) * C)
                r = rkv_ref[bi, rows, 0:D_A]
                k = rkv_ref[bi, rows, D_A:2 * D_A]
                v = rkv_ref[bi, rows, 2 * D_A:3 * D_A]
                vb_s[w0_slot, bi, rows, :] = v.astype(BF16)
                lo = lora_ref[bi, rows, :]
                wd = jnp.tanh(lo[:, 0:2 * LORA]).astype(BF16)
                ad = lo[:, 2 * LORA:4 * LORA].astype(BF16)
                ld = -DECAY_SCALE * _sigmoid(w0 + _dot(wd, wup_ref[...]))
                a = _sigmoid(a0 + _dot(ad, aup_ref[...]))
                kk = k * k_k
                kk2 = _head_sum(kk * kk, ones_ref[...])
                hi, lo2 = _split2(ld)
                cum = _dot(tri_ref[...], jnp.concatenate([hi, lo2], axis=1))
                yield

                cum = cum[:, 0:D_A] + cum[:, D_A:2 * D_A]
                kk = kk * lax.rsqrt(kk2 + L2_EPS)
                k_d = k * (1.0 + (a - 1.0) * k_a)
                tot = cum[last_row:last_row + 1, :]
                et = jnp.broadcast_to(jnp.exp(tot), (2 * C, D_A)).T
                gT_s[w0_slot, bi, c] = jnp.concatenate(
                    [jnp.where(low, et[(2 * p) * HEAD:(2 * p + 1) * HEAD, :], et[(2 * p + 1) * HEAD:(2 * p + 2) * HEAD, :])
                     for p in range(H_A // 2)], axis=1)
                e_neg = jnp.exp(-cum)
                at_s[w0_slot, bi, rows, :] = (-kk * jnp.exp(cum - ld)).astype(BF16)
                rt_s[w0_slot, bi, rows, :] = (r * jnp.exp(cum)).astype(BF16)
                bt = kk * a * e_neg
                kt = k_d * e_neg
                btT_s[w0_slot, bi, c] = jnp.concatenate([bt, bt], axis=0).T.astype(BF16)
                ktT_s[w0_slot, bi, c] = jnp.concatenate([kt, kt], axis=0).T.astype(BF16)
                yield

                if finalize:
                    a0f, k_af = pv2_ref[0:1, :], pv2_ref[1:2, :]
                    r_kf, r_kb = pv2_ref[2:3, :], pv2_ref[3:4, :]
                    a_f = _sigmoid(a0f + _dot(ad, aupf_ref[...]))
                    k_df = k * (1.0 + (a_f - 1.0) * k_af)
                    bon_s[w0_slot, bi, rows, :] = _head_sum(r * (k_df * r_kf + k_d * r_kb), ones_ref[...]) * v
                    yield

    prep_ticks = nbat * n_chunks * (3 if finalize else 2)

    n_sq = int(math.log2(C)) - 2
    units = [(bi, c) for bi in range(nbat) for c in range(n_chunks)]
    waves = [units[k:k + WAVE_CHUNKS] for k in range(0, len(units), WAVE_CHUNKS)]

    def matrix_stream():
        for wave in waves:
            chains = [(bi, c, g) for bi, c in wave for g in range(N_GROUPS)]

            def ld_(ref, bi, c, g):
                return ref[r1_slot, bi, c * C:(c + 1) * C, g * G:(g + 1) * G]

            def st_(ref, bi, c, g, val):
                ref[w1_slot, bi, c * C:(c + 1) * C, g * G:(g + 1) * G] = val

            nc = len(chains)
            a_b, a_k = [None] * nc, [None] * nc
            for n, (bi, c, g) in enumerate(chains):
                ar = jnp.concatenate([ld_(at_s, bi, c, g), ld_(rt_s, bi, c, g)], axis=0)
                a_b[n] = _dot(ar, _block_diag_t(btT_s[r1_slot, bi, c, g * G:(g + 1) * G, :], low))
                a_k[n] = _dot(ar, _block_diag_t(ktT_s[r1_slot, bi, c, g * G:(g + 1) * G, :], low))
            yield
            pb, t, av = [None] * nc, [None] * nc, [None] * nc
            for n, (bi, c, g) in enumerate(chains):
                a_ab = jnp.where(m_strict, a_b[n][0:C], 0.0)
                st_(arb_s, bi, c, g, jnp.where(m_incl, a_b[n][C:2 * C], 0.0).astype(BF16))
                a_kk = jnp.concatenate([jnp.where(m_strict, a_k[n][0:C], 0.0),
                                        jnp.where(m_incl, a_k[n][C:2 * C], 0.0)], axis=0).astype(BF16)
                ab = a_ab.astype(BF16)
                t[n] = eye + a_ab
                pb[n] = _dot(ab, _block_diag(ab, low))
                av[n] = _dot(a_kk, _block_diag(ld_(vb_s, bi, c, g), low))
            yield
            for s in range(n_sq):
                for n, (bi, c, g) in enumerate(chains):
                    if s == 0:
                        st_(avk_s, bi, c, g, av[n][0:C])
                        st_(y0_s, bi, c, g, av[n][C:2 * C])
                        p_n = pb[n].astype(BF16)
                    else:
                        p_n = pb[n][0:C].astype(BF16)
                        t[n] = t[n] + pb[n][C:2 * C]
                    pb[n] = _dot(jnp.concatenate([p_n, t[n].astype(BF16)], axis=0), _block_diag(p_n, low))
                yield
            for n, (bi, c, g) in enumerate(chains):
                p_n = pb[n][0:C].astype(BF16)
                t_n = t[n] + pb[n][C:2 * C]
                tf = t_n + _dot(t_n.astype(BF16), _block_diag(p_n, low))
                st_(tt_s, bi, c, g, tf.astype(BF16))
            yield

    matrix_ticks = len(waves) * (n_sq + 3)

    def state_stream(bi):
        if zero_init:
            s_scr[bi] = jnp.where(first2, 0.0, s_scr[bi])
        else:
            s_scr[bi] = jnp.where(first2, s0_ref[bi], s_scr[bi])
        yield
        sls = [slice(g * G, (g + 1) * G) for g in range(N_GROUPS)]
        for ci in range(n_chunks):
            c = (n_chunks - 1 - ci) if reverse else ci
            rows = slice(c * C, (c + 1) * C)
            s_c = [s_scr[bi, :, sl] for sl in sls]
            ar = [jnp.concatenate([at_s[r2_slot, bi, rows, sl], rt_s[r2_slot, bi, rows, sl]], axis=0) for sl in sls]
            xs_ = [_dot(ar[g], _block_diag(s_c[g].astype(BF16), low)) for g in range(N_GROUPS)]
            yield
            w = [xs_[g][0:C] + avk_s[q2_slot, bi, rows, sl] for g, sl in enumerate(sls)]
            u = [_dot(tt_s[q2_slot, bi, rows, sl], _block_diag(w[g].astype(BF16), low)) for g, sl in enumerate(sls)]
            yield
            ub = [x.astype(BF16) for x in u]
            full = [_dot(jnp.where(low_t, btT_s[r2_slot, bi, c, sl, :], ktT_s[r2_slot, bi, c, sl, :]),
                         jnp.concatenate([ub[g], vb_s[r2_slot, bi, rows, sl]], axis=0))
                    for g, sl in enumerate(sls)]
            yv = [_dot(arb_s[q2_slot, bi, rows, sl], _block_diag(ub[g], low)) for g, sl in enumerate(sls)]
            for g, sl in enumerate(sls):
                upd = jnp.zeros((HEAD, G), F32)
                for h in range(HEADS_PER_GROUP):
                    upd = jnp.where(hcol == h, full[g][h * HEAD:(h + 1) * HEAD, :], upd)
                s_scr[bi, :, sl] = (s_c[g] + upd) * gT_s[r2_slot, bi, c, :, sl]
                yd_ref[bi, rows, sl] = xs_[g][C:2 * C] + yv[g] + y0_s[q2_slot, bi, rows, sl]
            yield
        if finalize:
            gn_w, gn_b = pv2_ref[4:5, :], pv2_ref[5:6, :]
            for hlf in range(2):
                rows = slice(hlf * (tb // 2), (hlf + 1) * (tb // 2))
                yt = yf_ref[bi, rows, :] + yd_ref[bi, rows, :]
                mean = _head_sum(yt, ones_ref[...]) * (1.0 / HEAD)
                yc = yt - mean
                var = _head_sum(yc * yc, ones_ref[...]) * (1.0 / HEAD)
                yn = yc * lax.rsqrt(var + GN_EPS) * gn_w + gn_b + bon_s[r2_slot, bi, rows, :]
                z = za_ref[bi, rows, :]
                y_ref[bi, rows, :] = (yn * (z * _sigmoid(z))).astype(y_ref.dtype)
                yield

    state_ticks = 1 + 3 * n_chunks + (2 if finalize else 0)

    _interleave([(prep_stream(), prep_ticks), (matrix_stream(), matrix_ticks)]
                + [(state_stream(bi), state_ticks) for bi in range(nbat)])

    if emit_state:
        @pl.when(last2)
        def _():
            sout_ref[...] = s_scr[...]


def _scan(rkv, lora, s0c, consts, *, reverse, finalize, emit_state, nbat, tb, yf=None, za=None):
    nb, L, _ = rkv.shape
    nj = L // tb
    n_items = (nb // nbat) * nj
    zero_init = s0c is None
    d = 1 if reverse else 0

    def pos(item):
        jpos = lax.rem(item, nj)
        return item // nj, ((nj - 1 - jpos) if reverse else jpos)

    def item0(i):
        return jnp.minimum(i, n_items - 1)

    def item2(i):
        return jnp.clip(i - 2, 0, n_items - 1)

    def tok(w, item_of):
        def imap(i):
            p, jj = pos(item_of(i))
            return (p, jj, 0)
        return pl.BlockSpec((nbat, tb, w), imap)

    full2 = lambda a: pl.BlockSpec(a.shape, lambda i: (0, 0))
    st_spec = pl.BlockSpec((nbat, HEAD, D_A), lambda i: (pos(item2(i))[0], 0, 0))

    args = [rkv, lora]
    in_specs = [tok(D_RKV, item0), tok(D_LORA, item0)]
    if not zero_init:
        args.append(s0c)
        in_specs.append(st_spec)
    small = [consts["pv"][d], consts["wup"][d], consts["aup"][d], consts["tri"][d], consts["ones_bd"]]
    args += small
    in_specs += [full2(a) for a in small]
    if finalize:
        args += [yf, za, consts["pv2"], consts["aup"][0]]
        in_specs += [tok(D_A, item2), tok(D_A, item2), full2(consts["pv2"]), full2(consts["aup"][0])]

    out_shape = [jax.ShapeDtypeStruct((nb, L, D_A), BF16 if finalize else F32)]
    out_specs = [tok(D_A, item2)]
    if emit_state:
        out_shape.append(jax.ShapeDtypeStruct((nb, HEAD, D_A), F32))
        out_specs.append(st_spec)

    tokbuf = lambda slots, dt: pltpu.VMEM((slots, nbat, tb, D_A), dt)
    scratch = [pltpu.VMEM((nbat, HEAD, D_A), F32)]
    scratch += [tokbuf(3, BF16)] * 3
    scratch += [pltpu.VMEM((3, nbat, tb // SCAN_CHUNK, HEAD, D_A), F32)]
    scratch += [pltpu.VMEM((3, nbat, tb // SCAN_CHUNK, D_A, 2 * SCAN_CHUNK), BF16)] * 2
    if finalize:
        scratch += [tokbuf(3, F32)]
    scratch += [tokbuf(2, BF16)] * 2 + [tokbuf(2, F32)] * 2
    if finalize:
        scratch += [pltpu.VMEM((nbat, tb, D_A), F32)]

    kern = functools.partial(_scan_kernel, reverse=reverse, finalize=finalize, zero_init=zero_init,
                             emit_state=emit_state, nbat=nbat, tb=tb, nj=nj, n_items=n_items)
    outs = pl.pallas_call(
        kern,
        grid=(n_items + 2,),
        in_specs=in_specs,
        out_specs=out_specs,
        out_shape=out_shape,
        scratch_shapes=scratch,
        compiler_params=pltpu.CompilerParams(
            dimension_semantics=("arbitrary",), vmem_limit_bytes=VMEM_LIMIT),
        name="scan_bwd" if reverse else "scan_fwd",
    )(*args)
    return outs if emit_state else (outs[0], None)


def _out_kernel(x_ref, uvz_ref, ya_ref, mod_ref, lnpost_ref, sgu_ref, ws_ref, bs_ref, wout_ref, o_ref, *, tm):
    gate = mod_ref[0, 2:3, :]
    ln_g = sgu_ref[0:1, :]
    ln_b = sgu_ref[1:2, :]
    lane = lax.broadcasted_iota(jnp.int32, (GMLP_CHUNK, 128), 1)
    low = lane < (D_B // H_B)
    post_gain = lnpost_ref[...] * gate

    def chunk_stream(q):
        rows = slice(q * GMLP_CHUNK, (q + 1) * GMLP_CHUNK)
        vb = uvz_ref[0, rows, D_B:2 * D_B].astype(F32)
        mu = jnp.mean(vb, axis=-1, keepdims=True)
        vc = vb - mu
        var = jnp.mean(vc * vc, axis=-1, keepdims=True)
        vn = (vc * lax.rsqrt(var + NORM_EPS)) * ln_g + ln_b
        yield
        parts = []
        for pr in range(H_B // 2):
            vp = vn[:, pr * 128:(pr + 1) * 128]
            rhs = jnp.concatenate([jnp.where(low, vp, 0.0), jnp.where(low, 0.0, vp)], axis=0)
            parts.append(_dot(ws_ref[pr], rhs.astype(BF16)))
        yield
        u = uvz_ref[0, rows, 0:D_B].astype(F32)
        zb = uvz_ref[0, rows, 2 * D_B:3 * D_B].astype(F32)
        s = jnp.concatenate(parts, axis=1) + bs_ref[...]
        yb = u * s * (zb * _sigmoid(zb))
        mixed = jnp.concatenate([ya_ref[0, rows, :], yb.astype(BF16)], axis=1)
        out = _dot(mixed, wout_ref[...])
        yield
        ms = jnp.mean(out * out, axis=-1, keepdims=True)
        o_ref[0, rows, :] = x_ref[0, rows, :] + (out * lax.rsqrt(ms + NORM_EPS)) * post_gain
        yield

    n_q = tm // GMLP_CHUNK
    streams = [chunk_stream(q) for q in range(n_q)]
    for step in range(n_q + 3):
        for q in range(n_q):
            if 0 <= step - q < 4:
                next(streams[q])


def _out_stage(x, uvz, ya, mod, ln_post, sgu, ws_cat, bs_x, w_out_bf, tm):
    nb, L, _ = x.shape
    tok = lambda d: pl.BlockSpec((1, tm, d), lambda b, i: (b, i, 0))
    full = lambda a: pl.BlockSpec(a.shape, lambda b, i: (0,) * a.ndim)
    return pl.pallas_call(
        functools.partial(_out_kernel, tm=tm),
        grid=(nb, L // tm),
        in_specs=[tok(D_MODEL), tok(D_UVZ), tok(D_A),
                  pl.BlockSpec((1, 3, D_MODEL), lambda b, i: (b, 0, 0)),
                  full(ln_post), full(sgu), full(ws_cat), full(bs_x), full(w_out_bf)],
        out_specs=tok(D_MODEL),
        out_shape=jax.ShapeDtypeStruct((nb, L, D_MODEL), F32),
        compiler_params=pltpu.CompilerParams(
            dimension_semantics=("arbitrary", "arbitrary"), vmem_limit_bytes=VMEM_LIMIT),
        name="out_stage",
    )(x, uvz, ya, mod, ln_post, sgu, ws_cat, bs_x, w_out_bf)


def _layer(x, mod, s0f, s0b, lw, consts, *, n_seq, seq_len, emit_state):
    nb, ntok, _ = x.shape
    per_seq = lambda a: a.reshape(n_seq, seq_len, a.shape[-1])
    rkv, za, lora, uvz = _inproj(x, mod, lw["ln_pre"], lw["ts_mu"], lw["w_in_bf"], TOKEN_TILE, seq_len)
    rkv, za, lora = per_seq(rkv), per_seq(za), per_seq(lora)
    yf, sf = _scan(rkv, lora, s0f, consts, reverse=False, finalize=False,
                   emit_state=emit_state, nbat=SCAN_BATCH, tb=SCAN_BLOCK)
    ya, sb = _scan(rkv, lora, s0b, consts, reverse=True, finalize=True,
                   emit_state=emit_state, nbat=SCAN_BATCH, tb=SCAN_BLOCK, yf=yf, za=za)
    y = _out_stage(x, uvz, ya.reshape(nb, ntok, D_A), mod, lw["ln_post"], lw["sgu"], lw["ws_cat"], lw["bs_x"],
                   lw["w_out_bf"], OUT_TILE)
    return y, sf, sb


def _pad_rows(a, n):
    return jnp.concatenate([a, jnp.zeros((n - a.shape[0],) + a.shape[1:], a.dtype)], axis=0)


def _to_compact(s):
    b = s.shape[0]
    return jnp.transpose(s, (0, 3, 1, 2)).reshape(b, HEAD, D_A)


def _from_compact(sc):
    b = sc.shape[0]
    return jnp.transpose(sc.reshape(b, HEAD, H_A, HEAD), (0, 2, 3, 1))


def kernel(x_prompt, x_sample, c, state_fwd, state_bwd, c_ctx, ln_pre, ln_post, w_mod, b_mod, w_in, ts_mu, w0, w_up, a0, a_up, k_k, k_a, r_k, gn_w, gn_b, sgu_ln_g, sgu_ln_b, w_s, b_s, w_out):
    depth = w_in.shape[0]
    batch, seq, _ = x_prompt.shape
    dec_batch, dec_seq, _ = x_sample.shape

    ti = jnp.arange(SCAN_CHUNK)
    tri = [(ti[None, :] <= ti[:, None]).astype(BF16),
           (ti[None, :] >= ti[:, None]).astype(BF16)]
    gi = jnp.arange(GROUP_LANES) // HEAD
    ones_bd = (gi[:, None] == gi[None, :]).astype(BF16)

    cvecs = _pad_rows(jnp.concatenate([c_ctx[None, :], c], axis=0), 16)

    y_ctx = x_prompt.reshape(1, batch * seq, D_MODEL)
    y_lat = x_sample
    new_f, new_b = [], []
    for l in range(depth):
        mods = _modulation(cvecs, w_mod[l], b_mod[l][None, :]).reshape(16, 3, D_MODEL)
        mod_ctx = mods[0:1]
        mod_lat = mods[1:1 + dec_batch]

        zpad = jnp.zeros((LORA, D_A), F32)
        wup = [jnp.concatenate([w_up[l, 0], zpad], 0).astype(BF16),
               jnp.concatenate([zpad, w_up[l, 1]], 0).astype(BF16)]
        aup = [jnp.concatenate([a_up[l, 0], zpad], 0).astype(BF16),
               jnp.concatenate([zpad, a_up[l, 1]], 0).astype(BF16)]
        pv = [_pad_rows(jnp.stack([w0[l, d], a0[l, d], k_k[l, d], k_a[l, d]]), 8) for d in range(2)]
        pv2 = _pad_rows(jnp.stack([a0[l, 0], k_a[l, 0], r_k[l, 0].reshape(D_A), r_k[l, 1].reshape(D_A),
                                   gn_w[l], gn_b[l]]), 8)
        consts = {"pv": pv, "pv2": pv2, "wup": wup, "aup": aup, "tri": tri, "ones_bd": ones_bd}
        lw = {
            "ln_pre": ln_pre[l][None, :], "ln_post": ln_post[l][None, :], "ts_mu": ts_mu[l],
            "w_in_bf": w_in[l].astype(BF16), "w_out_bf": w_out[l].astype(BF16),
            "sgu": jnp.stack([sgu_ln_g[l], sgu_ln_b[l]]),
            "ws_cat": jnp.concatenate([w_s[l, 0::2], w_s[l, 1::2]], axis=2).astype(BF16),
            "bs_x": jnp.repeat(b_s[l].T, D_B // H_B, axis=1),
        }

        y_ctx, sf, sb = _layer(y_ctx, mod_ctx, None, None, lw, consts,
                               n_seq=batch, seq_len=seq, emit_state=True)
        new_f.append(_from_compact(sf))
        new_b.append(_from_compact(sb))

        y_lat, _, _ = _layer(y_lat, mod_lat, _to_compact(state_fwd[:, l]), _to_compact(state_bwd[:, l]),
                             lw, consts, n_seq=dec_batch, seq_len=dec_seq, emit_state=False)

    y_prompt = y_ctx.reshape(batch, seq, D_MODEL)
    new_state_fwd = jnp.stack(new_f, axis=1).astype(x_prompt.dtype)
    new_state_bwd = jnp.stack(new_b, axis=1).astype(x_prompt.dtype)
    return (y_prompt, y_lat, new_state_fwd, new_state_bwd)
```

```python
import functools
import math

import jax
import jax.numpy as jnp
from jax import lax
from jax.experimental import pallas as pl
from jax.experimental.pallas import tpu as pltpu

F32 = jnp.float32
BF16 = jnp.bfloat16

D_MODEL = 1024
D_A = 512
D_B = 512
HEAD = 64
H_A = D_A // HEAD
H_B = 8
LORA = 64
GMLP_CHUNK = 128
D_RKV = 3 * D_A
D_LORA = 4 * LORA
D_UVZ = 3 * D_B
D_IN = D_RKV + D_A + D_LORA + D_UVZ
NORM_EPS = 1e-6
GN_EPS = 6.4e-4
L2_EPS = 1e-12
DECAY_SCALE = math.exp(-0.5)

SCAN_CHUNK = 64
GROUP_LANES = 256
HEADS_PER_GROUP = GROUP_LANES // HEAD
N_GROUPS = D_A // GROUP_LANES
HALO = 8
SUBLANES = 8
WAVE_CHUNKS = 4
RING = 3
SCAN_BATCH = 2
SCAN_BLOCK = 256
TOKEN_TILE = 1024
OUT_TILE = 1024

VMEM_LIMIT = 56 * 1024 * 1024


def _dot(a, b):
    return jnp.dot(a, b, preferred_element_type=F32)


def _split2(x):
    hi = x.astype(BF16)
    lo = (x - hi.astype(F32)).astype(BF16)
    return hi, lo


def _sigmoid(x):
    return 1.0 / (1.0 + jnp.exp(-x))


def _mod_kernel(c_ref, w_ref, b_ref, o_ref):
    c = c_ref[...]
    s = c * _sigmoid(c)
    sh, sl = _split2(s)
    wh, wl = _split2(w_ref[...])
    o_ref[...] = _dot(sh, wh) + _dot(sh, wl) + _dot(sl, wh) + b_ref[...]


def _modulation(cvecs, w_mod, b_mod):
    n = cvecs.shape[0]
    nblk = 3
    return pl.pallas_call(
        _mod_kernel,
        grid=(nblk,),
        in_specs=[
            pl.BlockSpec((n, D_MODEL), lambda j: (0, 0)),
            pl.BlockSpec((D_MODEL, D_MODEL), lambda j: (0, j)),
            pl.BlockSpec((1, D_MODEL), lambda j: (0, j)),
        ],
        out_specs=pl.BlockSpec((n, D_MODEL), lambda j: (0, j)),
        out_shape=jax.ShapeDtypeStruct((n, 3 * D_MODEL), F32),
        compiler_params=pltpu.CompilerParams(vmem_limit_bytes=VMEM_LIMIT),
        name="modulation",
    )(cvecs, w_mod, b_mod)


def _inproj_kernel(x_ref, xp_ref, xn_ref, mod_ref, lnpre_ref, mu_ref, w_ref,
                   rkv_ref, za_ref, lora_ref, uvz_ref, *, tm, seq_len):
    i = pl.program_id(1)
    x = jnp.concatenate([xp_ref[0], x_ref[0], xn_ref[0]], axis=0)
    shift = mod_ref[0, 0:1, :]
    scale = mod_ref[0, 1:2, :]
    gain = lnpre_ref[...] * (1.0 + scale)
    ms = jnp.mean(x * x, axis=-1, keepdims=True)
    h = (x * lax.rsqrt(ms + NORM_EPS)) * gain + shift
    hb_all = h.astype(BF16)
    hb = hb_all[HALO:HALO + tm]

    o = D_RKV
    za_ref[0] = _dot(hb, w_ref[:, o:o + D_A]); o += D_A
    lora_ref[0] = _dot(hb, w_ref[:, o:o + D_LORA]); o += D_LORA
    uvz_ref[0] = _dot(hb, w_ref[:, o:o + D_UVZ]).astype(uvz_ref.dtype)

    rkv = _dot(hb_all, w_ref[:, 0:D_RKV])
    row = lax.broadcasted_iota(jnp.int32, (tm, D_RKV), 0)
    if seq_len >= tm:
        starts = jnp.logical_and(row == 0, lax.rem(i * tm, seq_len) == 0)
        ends = jnp.logical_and(row == tm - 1, lax.rem((i + 1) * tm, seq_len) == 0)
    else:
        starts = functools.reduce(jnp.logical_or, [row == r for r in range(0, tm, seq_len)])
        ends = functools.reduce(jnp.logical_or, [row == r + seq_len - 1 for r in range(0, tm, seq_len)])
    cur = rkv[HALO:HALO + tm]
    prv = rkv[HALO - 1:HALO - 1 + tm]
    nxt = rkv[HALO + 1:HALO + 1 + tm]
    prv = jnp.where(starts, 0.0, prv)
    nxt = jnp.where(ends, 0.0, nxt)
    mu0 = mu_ref[0:1, :]
    mu1 = mu_ref[1:2, :]
    rkv_ref[0] = (1.0 - mu0 - mu1) * cur + mu0 * prv + mu1 * nxt


def _inproj(x, mod, ln_pre, ts_mu, w_in_bf, tm, seq_len):
    nb, L, _ = x.shape
    hb = tm // HALO
    nh = L // HALO
    tok = lambda d: pl.BlockSpec((1, tm, d), lambda b, i: (b, i, 0))
    shp = lambda d, dt=F32: jax.ShapeDtypeStruct((nb, L, d), dt)
    return pl.pallas_call(
        functools.partial(_inproj_kernel, tm=tm, seq_len=seq_len),
        grid=(nb, L // tm),
        in_specs=[
            tok(D_MODEL),
            pl.BlockSpec((1, HALO, D_MODEL), lambda b, i: (b, jnp.maximum(i * hb - 1, 0), 0)),
            pl.BlockSpec((1, HALO, D_MODEL), lambda b, i: (b, jnp.minimum((i + 1) * hb, nh - 1), 0)),
            pl.BlockSpec((1, 3, D_MODEL), lambda b, i: (b, 0, 0)),
            pl.BlockSpec((1, D_MODEL), lambda b, i: (0, 0)),
            pl.BlockSpec((2, D_RKV), lambda b, i: (0, 0)),
            pl.BlockSpec((D_MODEL, D_IN), lambda b, i: (0, 0)),
        ],
        out_specs=[tok(D_RKV), tok(D_A), tok(D_LORA), tok(D_UVZ)],
        out_shape=[shp(D_RKV), shp(D_A), shp(D_LORA), shp(D_UVZ, BF16)],
        compiler_params=pltpu.CompilerParams(
            dimension_semantics=("arbitrary", "arbitrary"), vmem_limit_bytes=VMEM_LIMIT),
        name="inproj",
    )(x, x, x, mod, ln_pre, ts_mu, w_in_bf)


def _block_diag(x, low):
    z = jnp.zeros((x.shape[0], 128), x.dtype)
    blocks = []
    for h in range(HEADS_PER_GROUP):
        tile = x[:, (h // 2) * 128:(h // 2 + 1) * 128]
        kept = jnp.where(low, tile, 0) if h % 2 == 0 else jnp.where(low, 0, tile)
        blocks.append(jnp.concatenate([kept, z] if h < 2 else [z, kept], axis=1))
    return jnp.concatenate(blocks, axis=0)


def _block_diag_t(xt, low):
    z = jnp.zeros((HEAD, 128), xt.dtype)
    blocks = []
    for h in range(HEADS_PER_GROUP):
        tile = xt[h * HEAD:(h + 1) * HEAD, :]
        kept = jnp.where(low, tile, 0) if h % 2 == 0 else jnp.where(low, 0, tile)
        blocks.append(jnp.concatenate([kept, z] if h < 2 else [z, kept], axis=1))
    return jnp.concatenate(blocks, axis=0)


def _head_sum(x, ones_bd):
    rows = x.shape[0]
    xb = x.astype(BF16)
    parts = [xb[:, g * GROUP_LANES:(g + 1) * GROUP_LANES] for g in range(N_GROUPS)]
    s = _dot(jnp.concatenate(parts, axis=0), ones_bd)
    return jnp.concatenate([s[g * rows:(g + 1) * rows] for g in range(N_GROUPS)], axis=1)


def _interleave(threads):
    gens = [g for g, _ in threads]
    total = [n for _, n in threads]
    done = [0] * len(gens)
    alive = [True] * len(gens)
    while any(alive):
        k = min((i for i in range(len(gens)) if alive[i]), key=lambda i: done[i] / total[i])
        try:
            next(gens[k])
            done[k] += 1
        except StopIteration:
            alive[k] = False


def _scan_kernel(*refs, reverse, finalize, zero_init, emit_state, nbat, tb, nj, n_items):
    it = iter(refs)
    rkv_ref, lora_ref = next(it), next(it)
    s0_ref = None if zero_init else next(it)
    pv_ref, wup_ref, aup_ref = next(it), next(it), next(it)
    tri_ref, ones_ref = next(it), next(it)
    if finalize:
        yf_ref, za_ref, pv2_ref, aupf_ref = next(it), next(it), next(it), next(it)
    y_ref = next(it)
    sout_ref = next(it) if emit_state else None
    s_scr = next(it)
    at_s, rt_s, vb_s, gT_s = (next(it) for _ in range(4))
    btT_s, ktT_s = next(it), next(it)
    bon_s = next(it) if finalize else None
    tt_s, arb_s, avk_s, y0_s = (next(it) for _ in range(4))
    yd_ref = next(it) if finalize else y_ref

    C = SCAN_CHUNK
    G = GROUP_LANES
    n_chunks = tb // C

    i = pl.program_id(0)
    slot = {}
    item2 = jnp.clip(i - 2, 0, n_items - 1)
    jpos2 = lax.rem(item2, nj)
    first2 = jpos2 == 0
    last2 = jnp.logical_and(jpos2 == nj - 1, i >= 2)

    @pl.when(i == 0)
    def _():
        for ref in (at_s, rt_s, vb_s, gT_s, btT_s, ktT_s, tt_s, arb_s, avk_s, y0_s) + (
                (bon_s,) if finalize else ()):
            ref[...] = jnp.zeros_like(ref)
        s_scr[...] = jnp.zeros_like(s_scr)

    row = lax.broadcasted_iota(jnp.int32, (C, G), 0)
    col = lax.broadcasted_iota(jnp.int32, (C, G), 1)
    scol = col % HEAD
    hcol = col // HEAD
    if reverse:
        m_strict, m_incl = scol > row, scol >= row
    else:
        m_strict, m_incl = scol < row, scol <= row
    eye = jnp.where(scol == row, 1.0, 0.0).astype(F32)
    last_row = 0 if reverse else C - 1
    low = lax.broadcasted_iota(jnp.int32, (C, 128), 1) < HEAD
    low_t = lax.broadcasted_iota(jnp.int32, (G, 128), 1) < HEAD

    w0, a0, k_k, k_a = pv_ref[0:1, :], pv_ref[1:2, :], pv_ref[2:3, :], pv_ref[3:4, :]

    def prep_stream():
        for bi in range(nbat):
            for c in range(n_chunks):
                rows = slice(c * C, (c + 1) * C)
                r = rkv_ref[bi, rows, 0:D_A]
                k = rkv_ref[bi, rows, D_A:2 * D_A]
                v = rkv_ref[bi, rows, 2 * D_A:3 * D_A]
                vb_s[slot["prep"], bi, rows, :] = v.astype(BF16)
                lo = lora_ref[bi, rows, :]
                wd = jnp.tanh(lo[:, 0:2 * LORA]).astype(BF16)
                ad = lo[:, 2 * LORA:4 * LORA].astype(BF16)
                ld = -DECAY_SCALE * _sigmoid(w0 + _dot(wd, wup_ref[...]))
                a = _sigmoid(a0 + _dot(ad, aup_ref[...]))
                kk = k * k_k
                kk2 = _head_sum(kk * kk, ones_ref[...])
                hi, lo2 = _split2(ld)
                cum = _dot(tri_ref[...], jnp.concatenate([hi, lo2], axis=1))
                yield

                cum = cum[:, 0:D_A] + cum[:, D_A:2 * D_A]
                kk = kk * lax.rsqrt(kk2 + L2_EPS)
                k_d = k * (1.0 + (a - 1.0) * k_a)
                tot = cum[last_row:last_row + 1, :]
                et = jnp.broadcast_to(jnp.exp(tot), (2 * C, D_A)).T
                gT_s[slot["prep"], bi, c] = jnp.concatenate(
                    [jnp.where(low, et[(2 * p) * HEAD:(2 * p + 1) * HEAD, :], et[(2 * p + 1) * HEAD:(2 * p + 2) * HEAD, :])
                     for p in range(H_A // 2)], axis=1)
                e_neg = jnp.exp(-cum)
                at_s[slot["prep"], bi, rows, :] = (-kk * jnp.exp(cum - ld)).astype(BF16)
                rt_s[slot["prep"], bi, rows, :] = (r * jnp.exp(cum)).astype(BF16)
                bt = kk * a * e_neg
                kt = k_d * e_neg
                btT_s[slot["prep"], bi, c] = jnp.concatenate([bt, bt], axis=0).T.astype(BF16)
                ktT_s[slot["prep"], bi, c] = jnp.concatenate([kt, kt], axis=0).T.astype(BF16)
                yield

                if finalize:
                    a0f, k_af = pv2_ref[0:1, :], pv2_ref[1:2, :]
                    r_kf, r_kb = pv2_ref[2:3, :], pv2_ref[3:4, :]
                    a_f = _sigmoid(a0f + _dot(ad, aupf_ref[...]))
                    k_df = k * (1.0 + (a_f - 1.0) * k_af)
                    bon_s[slot["prep"], bi, rows, :] = _head_sum(r * (k_df * r_kf + k_d * r_kb), ones_ref[...]) * v
                    yield

    prep_ticks = nbat * n_chunks * (3 if finalize else 2)

    n_sq = int(math.log2(C)) - 2
    units = [(bi, c) for bi in range(nbat) for c in range(n_chunks)]
    waves = [units[k:k + WAVE_CHUNKS] for k in range(0, len(units), WAVE_CHUNKS)]

    def matrix_stream():
        for wave in waves:
            chains = [(bi, c, g) for bi, c in wave for g in range(N_GROUPS)]

            def ld_(ref, bi, c, g):
                return ref[slot["matrix"], bi, c * C:(c + 1) * C, g * G:(g + 1) * G]

            def st_(ref, bi, c, g, val):
                ref[slot["matrix"], bi, c * C:(c + 1) * C, g * G:(g + 1) * G] = val

            nc = len(chains)
            a_b, a_k = [None] * nc, [None] * nc
            for n, (bi, c, g) in enumerate(chains):
                ar = jnp.concatenate([ld_(at_s, bi, c, g), ld_(rt_s, bi, c, g)], axis=0)
                a_b[n] = _dot(ar, _block_diag_t(btT_s[slot["matrix"], bi, c, g * G:(g + 1) * G, :], low))
                a_k[n] = _dot(ar, _block_diag_t(ktT_s[slot["matrix"], bi, c, g * G:(g + 1) * G, :], low))
            yield
            pb, t, av = [None] * nc, [None] * nc, [None] * nc
            for n, (bi, c, g) in enumerate(chains):
                a_ab = jnp.where(m_strict, a_b[n][0:C], 0.0)
                st_(arb_s, bi, c, g, jnp.where(m_incl, a_b[n][C:2 * C], 0.0).astype(BF16))
                a_kk = jnp.concatenate([jnp.where(m_strict, a_k[n][0:C], 0.0),
                                        jnp.where(m_incl, a_k[n][C:2 * C], 0.0)], axis=0).astype(BF16)
                ab = a_ab.astype(BF16)
                t[n] = eye + a_ab
                pb[n] = _dot(ab, _block_diag(ab, low))
                av[n] = _dot(a_kk, _block_diag(ld_(vb_s, bi, c, g), low))
            yield
            for s in range(n_sq):
                for n, (bi, c, g) in enumerate(chains):
                    if s == 0:
                        st_(avk_s, bi, c, g, av[n][0:C])
                        st_(y0_s, bi, c, g, av[n][C:2 * C])
                        p_n = pb[n].astype(BF16)
                    else:
                        p_n = pb[n][0:C].astype(BF16)
                        t[n] = t[n] + pb[n][C:2 * C]
                    pb[n] = _dot(jnp.concatenate([p_n, t[n].astype(BF16)], axis=0), _block_diag(p_n, low))
                yield
            for n, (bi, c, g) in enumerate(chains):
                p_n = pb[n][0:C].astype(BF16)
                t_n = t[n] + pb[n][C:2 * C]
                tf = t_n + _dot(t_n.astype(BF16), _block_diag(p_n, low))
                st_(tt_s, bi, c, g, tf.astype(BF16))
            yield

    matrix_ticks = len(waves) * (n_sq + 3)

    def state_stream(bi):
        if zero_init:
            s_scr[bi] = jnp.where(first2, 0.0, s_scr[bi])
        else:
            s_scr[bi] = jnp.where(first2, s0_ref[bi], s_scr[bi])
        yield
        sls = [slice(g * G, (g + 1) * G) for g in range(N_GROUPS)]
        for ci in range(n_chunks):
            c = (n_chunks - 1 - ci) if reverse else ci
            rows = slice(c * C, (c + 1) * C)
            s_c = [s_scr[bi, :, sl] for sl in sls]
            ar = [jnp.concatenate([at_s[slot["state"], bi, rows, sl], rt_s[slot["state"], bi, rows, sl]], axis=0) for sl in sls]
            xs_ = [_dot(ar[g], _block_diag(s_c[g].astype(BF16), low)) for g in range(N_GROUPS)]
            yield
            w = [xs_[g][0:C] + avk_s[slot["state"], bi, rows, sl] for g, sl in enumerate(sls)]
            u = [_dot(tt_s[slot["state"], bi, rows, sl], _block_diag(w[g].astype(BF16), low)) for g, sl in enumerate(sls)]
            yield
            ub = [x.astype(BF16) for x in u]
            full = [_dot(jnp.where(low_t, btT_s[slot["state"], bi, c, sl, :], ktT_s[slot["state"], bi, c, sl, :]),
                         jnp.concatenate([ub[g], vb_s[slot["state"], bi, rows, sl]], axis=0))
                    for g, sl in enumerate(sls)]
            yv = [_dot(arb_s[slot["state"], bi, rows, sl], _block_diag(ub[g], low)) for g, sl in enumerate(sls)]
            for g, sl in enumerate(sls):
                upd = jnp.zeros((HEAD, G), F32)
                for h in range(HEADS_PER_GROUP):
                    upd = jnp.where(hcol == h, full[g][h * HEAD:(h + 1) * HEAD, :], upd)
                s_scr[bi, :, sl] = (s_c[g] + upd) * gT_s[slot["state"], bi, c, :, sl]
                yd_ref[bi, rows, sl] = xs_[g][C:2 * C] + yv[g] + y0_s[slot["state"], bi, rows, sl]
            yield
        if finalize:
            gn_w, gn_b = pv2_ref[4:5, :], pv2_ref[5:6, :]
            for hlf in range(2):
                rows = slice(hlf * (tb // 2), (hlf + 1) * (tb // 2))
                yt = yf_ref[bi, rows, :] + yd_ref[bi, rows, :]
                mean = _head_sum(yt, ones_ref[...]) * (1.0 / HEAD)
                yc = yt - mean
                var = _head_sum(yc * yc, ones_ref[...]) * (1.0 / HEAD)
                yn = yc * lax.rsqrt(var + GN_EPS) * gn_w + gn_b + bon_s[slot["state"], bi, rows, :]
                z = za_ref[bi, rows, :]
                y_ref[bi, rows, :] = (yn * (z * _sigmoid(z))).astype(y_ref.dtype)
                yield

    state_ticks = 1 + 3 * n_chunks + (2 if finalize else 0)

    def step_body(phase):
        slot.update(prep=phase, matrix=(phase + RING - 1) % RING, state=(phase + RING - 2) % RING)
        _interleave([(prep_stream(), prep_ticks), (matrix_stream(), matrix_ticks)]
                    + [(state_stream(bi), state_ticks) for bi in range(nbat)])

    for phase in range(RING):
        pl.when(lax.rem(i, RING) == phase)(functools.partial(step_body, phase))

    if emit_state:
        @pl.when(last2)
        def _():
            sout_ref[...] = s_scr[...]


def _scan(rkv, lora, s0c, consts, *, reverse, finalize, emit_state, nbat, tb, yf=None, za=None):
    nb, L, _ = rkv.shape
    nj = L // tb
    n_items = (nb // nbat) * nj
    zero_init = s0c is None
    d = 1 if reverse else 0

    def pos(item):
        jpos = lax.rem(item, nj)
        return item // nj, ((nj - 1 - jpos) if reverse else jpos)

    def item0(i):
        return jnp.minimum(i, n_items - 1)

    def item2(i):
        return jnp.clip(i - 2, 0, n_items - 1)

    def tok(w, item_of):
        def imap(i):
            p, jj = pos(item_of(i))
            return (p, jj, 0)
        return pl.BlockSpec((nbat, tb, w), imap)

    full2 = lambda a: pl.BlockSpec(a.shape, lambda i: (0, 0))
    st_spec = pl.BlockSpec((nbat, HEAD, D_A), lambda i: (pos(item2(i))[0], 0, 0))

    args = [rkv, lora]
    in_specs = [tok(D_RKV, item0), tok(D_LORA, item0)]
    if not zero_init:
        args.append(s0c)
        in_specs.append(st_spec)
    small = [consts["pv"][d], consts["wup"][d], consts["aup"][d], consts["tri"][d], consts["ones_bd"]]
    args += small
    in_specs += [full2(a) for a in small]
    if finalize:
        args += [yf, za, consts["pv2"], consts["aup"][0]]
        in_specs += [tok(D_A, item2), tok(D_A, item2), full2(consts["pv2"]), full2(consts["aup"][0])]

    out_shape = [jax.ShapeDtypeStruct((nb, L, D_A), BF16 if finalize else F32)]
    out_specs = [tok(D_A, item2)]
    if emit_state:
        out_shape.append(jax.ShapeDtypeStruct((nb, HEAD, D_A), F32))
        out_specs.append(st_spec)

    tokbuf = lambda dt: pltpu.VMEM((RING, nbat, tb, D_A), dt)
    scratch = [pltpu.VMEM((nbat, HEAD, D_A), F32)]
    scratch += [tokbuf(BF16)] * 3
    scratch += [pltpu.VMEM((RING, nbat, tb // SCAN_CHUNK, HEAD, D_A), F32)]
    scratch += [pltpu.VMEM((RING, nbat, tb // SCAN_CHUNK, D_A, 2 * SCAN_CHUNK), BF16)] * 2
    if finalize:
        scratch += [tokbuf(F32)]
    scratch += [tokbuf(BF16)] * 2 + [tokbuf(F32)] * 2
    if finalize:
        scratch += [pltpu.VMEM((nbat, tb, D_A), F32)]

    kern = functools.partial(_scan_kernel, reverse=reverse, finalize=finalize, zero_init=zero_init,
                             emit_state=emit_state, nbat=nbat, tb=tb, nj=nj, n_items=n_items)
    outs = pl.pallas_call(
        kern,
        grid=(n_items + 2,),
        in_specs=in_specs,
        out_specs=out_specs,
        out_shape=out_shape,
        scratch_shapes=scratch,
        compiler_params=pltpu.CompilerParams(
            dimension_semantics=("arbitrary",), vmem_limit_bytes=VMEM_LIMIT),
        name="scan_bwd" if reverse else "scan_fwd",
    )(*args)
    return outs if emit_state else (outs[0], None)


def _out_kernel(x_ref, uvz_ref, ya_ref, mod_ref, lnpost_ref, sgu_ref, ws_ref, bs_ref, wout_ref, o_ref, *, tm):
    gate = mod_ref[0, 2:3, :]
    ln_g = sgu_ref[0:1, :]
    ln_b = sgu_ref[1:2, :]
    lane = lax.broadcasted_iota(jnp.int32, (GMLP_CHUNK, 128), 1)
    low = lane < (D_B // H_B)
    post_gain = lnpost_ref[...] * gate

    def chunk_stream(q):
        rows = slice(q * GMLP_CHUNK, (q + 1) * GMLP_CHUNK)
        vb = uvz_ref[0, rows, D_B:2 * D_B].astype(F32)
        mu = jnp.mean(vb, axis=-1, keepdims=True)
        vc = vb - mu
        var = jnp.mean(vc * vc, axis=-1, keepdims=True)
        vn = (vc * lax.rsqrt(var + NORM_EPS)) * ln_g + ln_b
        yield
        parts = []
        for pr in range(H_B // 2):
            vp = vn[:, pr * 128:(pr + 1) * 128]
            rhs = jnp.concatenate([jnp.where(low, vp, 0.0), jnp.where(low, 0.0, vp)], axis=0)
            parts.append(_dot(ws_ref[pr], rhs.astype(BF16)))
        yield
        u = uvz_ref[0, rows, 0:D_B].astype(F32)
        zb = uvz_ref[0, rows, 2 * D_B:3 * D_B].astype(F32)
        s = jnp.concatenate(parts, axis=1) + bs_ref[...]
        yb = u * s * (zb * _sigmoid(zb))
        mixed = jnp.concatenate([ya_ref[0, rows, :], yb.astype(BF16)], axis=1)
        out = _dot(mixed, wout_ref[...])
        yield
        ms = jnp.mean(out * out, axis=-1, keepdims=True)
        o_ref[0, rows, :] = x_ref[0, rows, :] + (out * lax.rsqrt(ms + NORM_EPS)) * post_gain
        yield

    n_q = tm // GMLP_CHUNK
    streams = [chunk_stream(q) for q in range(n_q)]
    for step in range(n_q + 3):
        for q in range(n_q):
            if 0 <= step - q < 4:
                next(streams[q])


def _out_stage(x, uvz, ya, mod, ln_post, sgu, ws_cat, bs_x, w_out_bf, tm):
    nb, L, _ = x.shape
    tok = lambda d: pl.BlockSpec((1, tm, d), lambda b, i: (b, i, 0))
    full = lambda a: pl.BlockSpec(a.shape, lambda b, i: (0,) * a.ndim)
    return pl.pallas_call(
        functools.partial(_out_kernel, tm=tm),
        grid=(nb, L // tm),
        in_specs=[tok(D_MODEL), tok(D_UVZ), tok(D_A),
                  pl.BlockSpec((1, 3, D_MODEL), lambda b, i: (b, 0, 0)),
                  full(ln_post), full(sgu), full(ws_cat), full(bs_x), full(w_out_bf)],
        out_specs=tok(D_MODEL),
        out_shape=jax.ShapeDtypeStruct((nb, L, D_MODEL), F32),
        compiler_params=pltpu.CompilerParams(
            dimension_semantics=("arbitrary", "arbitrary"), vmem_limit_bytes=VMEM_LIMIT),
        name="out_stage",
    )(x, uvz, ya, mod, ln_post, sgu, ws_cat, bs_x, w_out_bf)


def _layer(x, mod, s0f, s0b, lw, consts, *, n_seq, seq_len, emit_state):
    nb, ntok, _ = x.shape
    per_seq = lambda a: a.reshape(n_seq, seq_len, a.shape[-1])
    rkv, za, lora, uvz = _inproj(x, mod, lw["ln_pre"], lw["ts_mu"], lw["w_in_bf"], TOKEN_TILE, seq_len)
    rkv, za, lora = per_seq(rkv), per_seq(za), per_seq(lora)
    yf, sf = _scan(rkv, lora, s0f, consts, reverse=False, finalize=False,
                   emit_state=emit_state, nbat=SCAN_BATCH, tb=SCAN_BLOCK)
    ya, sb = _scan(rkv, lora, s0b, consts, reverse=True, finalize=True,
                   emit_state=emit_state, nbat=SCAN_BATCH, tb=SCAN_BLOCK, yf=yf, za=za)
    y = _out_stage(x, uvz, ya.reshape(nb, ntok, D_A), mod, lw["ln_post"], lw["sgu"], lw["ws_cat"], lw["bs_x"],
                   lw["w_out_bf"], OUT_TILE)
    return y, sf, sb


def _pad_rows(a, n):
    return jnp.concatenate([a, jnp.zeros((n - a.shape[0],) + a.shape[1:], a.dtype)], axis=0)


def _to_compact(s):
    b = s.shape[0]
    return jnp.transpose(s, (0, 3, 1, 2)).reshape(b, HEAD, D_A)


def _from_compact(sc):
    b = sc.shape[0]
    return jnp.transpose(sc.reshape(b, HEAD, H_A, HEAD), (0, 2, 3, 1))


def kernel(x_prompt, x_sample, c, state_fwd, state_bwd, c_ctx, ln_pre, ln_post, w_mod, b_mod, w_in, ts_mu, w0, w_up, a0, a_up, k_k, k_a, r_k, gn_w, gn_b, sgu_ln_g, sgu_ln_b, w_s, b_s, w_out):
    depth = w_in.shape[0]
    batch, seq, _ = x_prompt.shape
    dec_batch, dec_seq, _ = x_sample.shape

    ti = jnp.arange(SCAN_CHUNK)
    tri = [(ti[None, :] <= ti[:, None]).astype(BF16),
           (ti[None, :] >= ti[:, None]).astype(BF16)]
    gi = jnp.arange(GROUP_LANES) // HEAD
    ones_bd = (gi[:, None] == gi[None, :]).astype(BF16)

    cvecs = _pad_rows(jnp.concatenate([c_ctx[None, :], c], axis=0), 16)

    y_ctx = x_prompt.reshape(1, batch * seq, D_MODEL)
    y_lat = x_sample
    new_f, new_b = [], []
    for l in range(depth):
        mods = _modulation(cvecs, w_mod[l], b_mod[l][None, :]).reshape(16, 3, D_MODEL)
        mod_ctx = mods[0:1]
        mod_lat = mods[1:1 + dec_batch]

        zpad = jnp.zeros((LORA, D_A), F32)
        wup = [jnp.concatenate([w_up[l, 0], zpad], 0).astype(BF16),
               jnp.concatenate([zpad, w_up[l, 1]], 0).astype(BF16)]
        aup = [jnp.concatenate([a_up[l, 0], zpad], 0).astype(BF16),
               jnp.concatenate([zpad, a_up[l, 1]], 0).astype(BF16)]
        pv = [_pad_rows(jnp.stack([w0[l, d], a0[l, d], k_k[l, d], k_a[l, d]]), 8) for d in range(2)]
        pv2 = _pad_rows(jnp.stack([a0[l, 0], k_a[l, 0], r_k[l, 0].reshape(D_A), r_k[l, 1].reshape(D_A),
                                   gn_w[l], gn_b[l]]), 8)
        consts = {"pv": pv, "pv2": pv2, "wup": wup, "aup": aup, "tri": tri, "ones_bd": ones_bd}
        lw = {
            "ln_pre": ln_pre[l][None, :], "ln_post": ln_post[l][None, :], "ts_mu": ts_mu[l],
            "w_in_bf": w_in[l].astype(BF16), "w_out_bf": w_out[l].astype(BF16),
            "sgu": jnp.stack([sgu_ln_g[l], sgu_ln_b[l]]),
            "ws_cat": jnp.concatenate([w_s[l, 0::2], w_s[l, 1::2]], axis=2).astype(BF16),
            "bs_x": jnp.repeat(b_s[l].T, D_B // H_B, axis=1),
        }

        y_ctx, sf, sb = _layer(y_ctx, mod_ctx, None, None, lw, consts,
                               n_seq=batch, seq_len=seq, emit_state=True)
        new_f.append(_from_compact(sf))
        new_b.append(_from_compact(sb))

        y_lat, _, _ = _layer(y_lat, mod_lat, _to_compact(state_fwd[:, l]), _to_compact(state_bwd[:, l]),
                             lw, consts, n_seq=dec_batch, seq_len=dec_seq, emit_state=False)

    y_prompt = y_ctx.reshape(batch, seq, D_MODEL)
    new_state_fwd = jnp.stack(new_f, axis=1).astype(x_prompt.dtype)
    new_state_bwd = jnp.stack(new_b, axis=1).astype(x_prompt.dtype)
    return (y_prompt, y_lat, new_state_fwd, new_state_bwd)
```

```python
import functools
import math

import jax
import jax.numpy as jnp
from jax import lax
from jax.experimental import pallas as pl
from jax.experimental.pallas import tpu as pltpu

F32 = jnp.float32
BF16 = jnp.bfloat16

D_MODEL = 1024
D_A = 512
D_B = 512
HEAD = 64
H_A = D_A // HEAD
H_B = 8
LORA = 64
GMLP_CHUNK = 128
D_RKV = 3 * D_A
D_LORA = 4 * LORA
D_UVZ = 3 * D_B
D_IN = D_RKV + D_A + D_LORA + D_UVZ
NORM_EPS = 1e-6
GN_EPS = 6.4e-4
L2_EPS = 1e-12
DECAY_SCALE = math.exp(-0.5)

SCAN_CHUNK = 64
GROUP_LANES = 256
HEADS_PER_GROUP = GROUP_LANES // HEAD
N_GROUPS = D_A // GROUP_LANES
HALO = 8
LANES = 128
WAVE_CHUNKS = 4
RING = 3
STATIC_RING_MIN_ITEMS = 32
SCAN_BATCH = 2
SCAN_BLOCK = 256
TOKEN_TILE = 1024
OUT_TILE = 1024

VMEM_LIMIT = 56 * 1024 * 1024


def _dot(a, b):
    return jnp.dot(a, b, preferred_element_type=F32)


def _split2(x):
    hi = x.astype(BF16)
    lo = (x - hi.astype(F32)).astype(BF16)
    return hi, lo


def _sigmoid(x):
    return 1.0 / (1.0 + jnp.exp(-x))


def _mod_kernel(c_ref, w_ref, b_ref, o_ref):
    c = c_ref[...]
    s = c * _sigmoid(c)
    sh, sl = _split2(s)
    wh, wl = _split2(w_ref[...])
    o_ref[...] = _dot(sh, wh) + _dot(sh, wl) + _dot(sl, wh) + b_ref[...]


def _modulation(cvecs, w_mod, b_mod):
    n = cvecs.shape[0]
    nblk = 3
    return pl.pallas_call(
        _mod_kernel,
        grid=(nblk,),
        in_specs=[
            pl.BlockSpec((n, D_MODEL), lambda j: (0, 0)),
            pl.BlockSpec((D_MODEL, D_MODEL), lambda j: (0, j)),
            pl.BlockSpec((1, D_MODEL), lambda j: (0, j)),
        ],
        out_specs=pl.BlockSpec((n, D_MODEL), lambda j: (0, j)),
        out_shape=jax.ShapeDtypeStruct((n, 3 * D_MODEL), F32),
        compiler_params=pltpu.CompilerParams(vmem_limit_bytes=VMEM_LIMIT),
        name="modulation",
    )(cvecs, w_mod, b_mod)


def _inproj_kernel(x_ref, xp_ref, xn_ref, mod_ref, lnpre_ref, mu_ref, w_ref,
                   rkv_ref, za_ref, lora_ref, uvz_ref, *, tm, seq_len):
    i = pl.program_id(1)
    x = jnp.concatenate([xp_ref[0], x_ref[0], xn_ref[0]], axis=0)
    shift = mod_ref[0, 0:1, :]
    scale = mod_ref[0, 1:2, :]
    gain = lnpre_ref[...] * (1.0 + scale)
    ms = jnp.mean(x * x, axis=-1, keepdims=True)
    h = (x * lax.rsqrt(ms + NORM_EPS)) * gain + shift
    hb_all = h.astype(BF16)
    hb = hb_all[HALO:HALO + tm]

    o = D_RKV
    za_ref[0] = _dot(hb, w_ref[:, o:o + D_A]); o += D_A
    lora_ref[0] = _dot(hb, w_ref[:, o:o + D_LORA]); o += D_LORA
    uvz_ref[0] = _dot(hb, w_ref[:, o:o + D_UVZ]).astype(uvz_ref.dtype)

    rkv = _dot(hb_all, w_ref[:, 0:D_RKV])
    row = lax.broadcasted_iota(jnp.int32, (tm, D_RKV), 0)
    if seq_len >= tm:
        starts = jnp.logical_and(row == 0, lax.rem(i * tm, seq_len) == 0)
        ends = jnp.logical_and(row == tm - 1, lax.rem((i + 1) * tm, seq_len) == 0)
    else:
        starts = functools.reduce(jnp.logical_or, [row == r for r in range(0, tm, seq_len)])
        ends = functools.reduce(jnp.logical_or, [row == r + seq_len - 1 for r in range(0, tm, seq_len)])
    cur = rkv[HALO:HALO + tm]
    prv = rkv[HALO - 1:HALO - 1 + tm]
    nxt = rkv[HALO + 1:HALO + 1 + tm]
    prv = jnp.where(starts, 0.0, prv)
    nxt = jnp.where(ends, 0.0, nxt)
    mu0 = mu_ref[0:1, :]
    mu1 = mu_ref[1:2, :]
    rkv_ref[0] = (1.0 - mu0 - mu1) * cur + mu0 * prv + mu1 * nxt


def _inproj(x, mod, ln_pre, ts_mu, w_in_bf, tm, seq_len):
    nb, L, _ = x.shape
    hb = tm // HALO
    nh = L // HALO
    tok = lambda d: pl.BlockSpec((1, tm, d), lambda b, i: (b, i, 0))
    shp = lambda d, dt=F32: jax.ShapeDtypeStruct((nb, L, d), dt)
    return pl.pallas_call(
        functools.partial(_inproj_kernel, tm=tm, seq_len=seq_len),
        grid=(nb, L // tm),
        in_specs=[
            tok(D_MODEL),
            pl.BlockSpec((1, HALO, D_MODEL), lambda b, i: (b, jnp.maximum(i * hb - 1, 0), 0)),
            pl.BlockSpec((1, HALO, D_MODEL), lambda b, i: (b, jnp.minimum((i + 1) * hb, nh - 1), 0)),
            pl.BlockSpec((1, 3, D_MODEL), lambda b, i: (b, 0, 0)),
            pl.BlockSpec((1, D_MODEL), lambda b, i: (0, 0)),
            pl.BlockSpec((2, D_RKV), lambda b, i: (0, 0)),
            pl.BlockSpec((D_MODEL, D_IN), lambda b, i: (0, 0)),
        ],
        out_specs=[tok(D_RKV), tok(D_A), tok(D_LORA), tok(D_UVZ)],
        out_shape=[shp(D_RKV), shp(D_A), shp(D_LORA), shp(D_UVZ, BF16)],
        compiler_params=pltpu.CompilerParams(
            dimension_semantics=("arbitrary", "arbitrary"), vmem_limit_bytes=VMEM_LIMIT),
        name="inproj",
    )(x, x, x, mod, ln_pre, ts_mu, w_in_bf)


def _block_diag(x, low):
    z = jnp.zeros((x.shape[0], LANES), x.dtype)
    blocks = []
    for h in range(HEADS_PER_GROUP):
        tile = x[:, (h // 2) * LANES:(h // 2 + 1) * LANES]
        kept = jnp.where(low, tile, 0) if h % 2 == 0 else jnp.where(low, 0, tile)
        blocks.append(jnp.concatenate([kept, z] if h < 2 else [z, kept], axis=1))
    return jnp.concatenate(blocks, axis=0)


def _block_diag_t(xt, low):
    z = jnp.zeros((HEAD, LANES), xt.dtype)
    blocks = []
    for h in range(HEADS_PER_GROUP):
        tile = xt[h * HEAD:(h + 1) * HEAD, :]
        kept = jnp.where(low, tile, 0) if h % 2 == 0 else jnp.where(low, 0, tile)
        blocks.append(jnp.concatenate([kept, z] if h < 2 else [z, kept], axis=1))
    return jnp.concatenate(blocks, axis=0)


def _head_sum(x, ones_bd):
    rows = x.shape[0]
    xb = x.astype(BF16)
    parts = [xb[:, g * GROUP_LANES:(g + 1) * GROUP_LANES] for g in range(N_GROUPS)]
    s = _dot(jnp.concatenate(parts, axis=0), ones_bd)
    return jnp.concatenate([s[g * rows:(g + 1) * rows] for g in range(N_GROUPS)], axis=1)


def _interleave(threads):
    gens = [g for g, _ in threads]
    total = [n for _, n in threads]
    done = [0] * len(gens)
    alive = [True] * len(gens)
    while any(alive):
        k = min((i for i in range(len(gens)) if alive[i]), key=lambda i: done[i] / total[i])
        try:
            next(gens[k])
            done[k] += 1
        except StopIteration:
            alive[k] = False


def _scan_kernel(*refs, reverse, finalize, zero_init, emit_state, nbat, tb, nj, n_items, static_ring):
    it = iter(refs)
    rkv_ref, lora_ref = next(it), next(it)
    s0_ref = None if zero_init else next(it)
    pv_ref, wup_ref, aup_ref = next(it), next(it), next(it)
    tri_ref, ones_ref = next(it), next(it)
    if finalize:
        yf_ref, za_ref, pv2_ref, aupf_ref = next(it), next(it), next(it), next(it)
    y_ref = next(it)
    sout_ref = next(it) if emit_state else None
    s_scr = next(it)
    at_s, rt_s, vb_s, gT_s = (next(it) for _ in range(4))
    btT_s, ktT_s = next(it), next(it)
    bon_s = next(it) if finalize else None
    tt_s, arb_s, avk_s, y0_s = (next(it) for _ in range(4))
    yd_ref = next(it) if finalize else y_ref

    C = SCAN_CHUNK
    G = GROUP_LANES
    n_chunks = tb // C

    i = pl.program_id(0)
    slot = {}
    item2 = jnp.clip(i - 2, 0, n_items - 1)
    jpos2 = lax.rem(item2, nj)
    first2 = jpos2 == 0
    last2 = jnp.logical_and(jpos2 == nj - 1, i >= 2)

    @pl.when(i == 0)
    def _():
        for ref in (at_s, rt_s, vb_s, gT_s, btT_s, ktT_s, tt_s, arb_s, avk_s, y0_s) + (
                (bon_s,) if finalize else ()):
            ref[...] = jnp.zeros_like(ref)
        s_scr[...] = jnp.zeros_like(s_scr)

    row = lax.broadcasted_iota(jnp.int32, (C, G), 0)
    col = lax.broadcasted_iota(jnp.int32, (C, G), 1)
    scol = col % HEAD
    hcol = col // HEAD
    if reverse:
        m_strict, m_incl = scol > row, scol >= row
    else:
        m_strict, m_incl = scol < row, scol <= row
    eye = jnp.where(scol == row, 1.0, 0.0).astype(F32)
    last_row = 0 if reverse else C - 1
    low = lax.broadcasted_iota(jnp.int32, (C, LANES), 1) < HEAD
    low_t = lax.broadcasted_iota(jnp.int32, (G, LANES), 1) < HEAD

    w0, a0, k_k, k_a = pv_ref[0:1, :], pv_ref[1:2, :], pv_ref[2:3, :], pv_ref[3:4, :]

    def prep_stream():
        for bi in range(nbat):
            for c in range(n_chunks):
                rows = slice(c * C, (c + 1) * C)
                r = rkv_ref[bi, rows, 0:D_A]
                k = rkv_ref[bi, rows, D_A:2 * D_A]
                v = rkv_ref[bi, rows, 2 * D_A:3 * D_A]
                vb_s[slot["prep"], bi, rows, :] = v.astype(BF16)
                lo = lora_ref[bi, rows, :]
                wd = jnp.tanh(lo[:, 0:2 * LORA]).astype(BF16)
                ad = lo[:, 2 * LORA:4 * LORA].astype(BF16)
                ld = -DECAY_SCALE * _sigmoid(w0 + _dot(wd, wup_ref[...]))
                a = _sigmoid(a0 + _dot(ad, aup_ref[...]))
                kk = k * k_k
                kk2 = _head_sum(kk * kk, ones_ref[...])
                hi, lo2 = _split2(ld)
                cum = _dot(tri_ref[...], jnp.concatenate([hi, lo2], axis=1))
                yield

                cum = cum[:, 0:D_A] + cum[:, D_A:2 * D_A]
                kk = kk * lax.rsqrt(kk2 + L2_EPS)
                k_d = k * (1.0 + (a - 1.0) * k_a)
                tot = cum[last_row:last_row + 1, :]
                et = jnp.broadcast_to(jnp.exp(tot), (2 * C, D_A)).T
                gT_s[slot["prep"], bi, c] = jnp.concatenate(
                    [jnp.where(low, et[(2 * p) * HEAD:(2 * p + 1) * HEAD, :], et[(2 * p + 1) * HEAD:(2 * p + 2) * HEAD, :])
                     for p in range(H_A // 2)], axis=1)
                e_neg = jnp.exp(-cum)
                at_s[slot["prep"], bi, rows, :] = (-kk * jnp.exp(cum - ld)).astype(BF16)
                rt_s[slot["prep"], bi, rows, :] = (r * jnp.exp(cum)).astype(BF16)
                bt = kk * a * e_neg
                kt = k_d * e_neg
                btT_s[slot["prep"], bi, c] = jnp.concatenate([bt, bt], axis=0).T.astype(BF16)
                ktT_s[slot["prep"], bi, c] = jnp.concatenate([kt, kt], axis=0).T.astype(BF16)
                yield

                if finalize:
                    a0f, k_af = pv2_ref[0:1, :], pv2_ref[1:2, :]
                    r_kf, r_kb = pv2_ref[2:3, :], pv2_ref[3:4, :]
                    a_f = _sigmoid(a0f + _dot(ad, aupf_ref[...]))
                    k_df = k * (1.0 + (a_f - 1.0) * k_af)
                    bon_s[slot["prep"], bi, rows, :] = _head_sum(r * (k_df * r_kf + k_d * r_kb), ones_ref[...]) * v
                    yield

    prep_ticks = nbat * n_chunks * (3 if finalize else 2)

    n_sq = int(math.log2(C)) - 2
    units = [(bi, c) for bi in range(nbat) for c in range(n_chunks)]
    waves = [units[k:k + WAVE_CHUNKS] for k in range(0, len(units), WAVE_CHUNKS)]

    def matrix_stream():
        for wave in waves:
            chains = [(bi, c, g) for bi, c in wave for g in range(N_GROUPS)]

            def ld_(ref, bi, c, g):
                return ref[slot["matrix"], bi, c * C:(c + 1) * C, g * G:(g + 1) * G]

            def st_(ref, bi, c, g, val):
                ref[slot["matrix"], bi, c * C:(c + 1) * C, g * G:(g + 1) * G] = val

            nc = len(chains)
            a_b, a_k = [None] * nc, [None] * nc
            for n, (bi, c, g) in enumerate(chains):
                ar = jnp.concatenate([ld_(at_s, bi, c, g), ld_(rt_s, bi, c, g)], axis=0)
                a_b[n] = _dot(ar, _block_diag_t(btT_s[slot["matrix"], bi, c, g * G:(g + 1) * G, :], low))
                a_k[n] = _dot(ar, _block_diag_t(ktT_s[slot["matrix"], bi, c, g * G:(g + 1) * G, :], low))
            yield
            pb, t, av = [None] * nc, [None] * nc, [None] * nc
            for n, (bi, c, g) in enumerate(chains):
                a_ab = jnp.where(m_strict, a_b[n][0:C], 0.0)
                st_(arb_s, bi, c, g, jnp.where(m_incl, a_b[n][C:2 * C], 0.0).astype(BF16))
                a_kk = jnp.concatenate([jnp.where(m_strict, a_k[n][0:C], 0.0),
                                        jnp.where(m_incl, a_k[n][C:2 * C], 0.0)], axis=0).astype(BF16)
                ab = a_ab.astype(BF16)
                t[n] = eye + a_ab
                pb[n] = _dot(ab, _block_diag(ab, low))
                av[n] = _dot(a_kk, _block_diag(ld_(vb_s, bi, c, g), low))
            yield
            for s in range(n_sq):
                for n, (bi, c, g) in enumerate(chains):
                    if s == 0:
                        st_(avk_s, bi, c, g, av[n][0:C])
                        st_(y0_s, bi, c, g, av[n][C:2 * C])
                        p_n = pb[n].astype(BF16)
                    else:
                        p_n = pb[n][0:C].astype(BF16)
                        t[n] = t[n] + pb[n][C:2 * C]
                    pb[n] = _dot(jnp.concatenate([p_n, t[n].astype(BF16)], axis=0), _block_diag(p_n, low))
                yield
            for n, (bi, c, g) in enumerate(chains):
                p_n = pb[n][0:C].astype(BF16)
                t_n = t[n] + pb[n][C:2 * C]
                tf = t_n + _dot(t_n.astype(BF16), _block_diag(p_n, low))
                st_(tt_s, bi, c, g, tf.astype(BF16))
            yield

    matrix_ticks = len(waves) * (n_sq + 3)

    def state_stream(bi):
        if zero_init:
            s_scr[bi] = jnp.where(first2, 0.0, s_scr[bi])
        else:
            s_scr[bi] = jnp.where(first2, s0_ref[bi], s_scr[bi])
        yield
        sls = [slice(g * G, (g + 1) * G) for g in range(N_GROUPS)]
        for ci in range(n_chunks):
            c = (n_chunks - 1 - ci) if reverse else ci
            rows = slice(c * C, (c + 1) * C)
            s_c = [s_scr[bi, :, sl] for sl in sls]
            ar = [jnp.concatenate([at_s[slot["state"], bi, rows, sl], rt_s[slot["state"], bi, rows, sl]], axis=0) for sl in sls]
            xs_ = [_dot(ar[g], _block_diag(s_c[g].astype(BF16), low)) for g in range(N_GROUPS)]
            yield
            w = [xs_[g][0:C] + avk_s[slot["state"], bi, rows, sl] for g, sl in enumerate(sls)]
            u = [_dot(tt_s[slot["state"], bi, rows, sl], _block_diag(w[g].astype(BF16), low)) for g, sl in enumerate(sls)]
            yield
            ub = [x.astype(BF16) for x in u]
            full = [_dot(jnp.where(low_t, btT_s[slot["state"], bi, c, sl, :], ktT_s[slot["state"], bi, c, sl, :]),
                         jnp.concatenate([ub[g], vb_s[slot["state"], bi, rows, sl]], axis=0))
                    for g, sl in enumerate(sls)]
            yv = [_dot(arb_s[slot["state"], bi, rows, sl], _block_diag(ub[g], low)) for g, sl in enumerate(sls)]
            for g, sl in enumerate(sls):
                upd = jnp.zeros((HEAD, G), F32)
                for h in range(HEADS_PER_GROUP):
                    upd = jnp.where(hcol == h, full[g][h * HEAD:(h + 1) * HEAD, :], upd)
                s_scr[bi, :, sl] = (s_c[g] + upd) * gT_s[slot["state"], bi, c, :, sl]
                yd_ref[bi, rows, sl] = xs_[g][C:2 * C] + yv[g] + y0_s[slot["state"], bi, rows, sl]
            yield
        if finalize:
            gn_w, gn_b = pv2_ref[4:5, :], pv2_ref[5:6, :]
            for hlf in range(2):
                rows = slice(hlf * (tb // 2), (hlf + 1) * (tb // 2))
                yt = yf_ref[bi, rows, :] + yd_ref[bi, rows, :]
                mean = _head_sum(yt, ones_ref[...]) * (1.0 / HEAD)
                yc = yt - mean
                var = _head_sum(yc * yc, ones_ref[...]) * (1.0 / HEAD)
                yn = yc * lax.rsqrt(var + GN_EPS) * gn_w + gn_b + bon_s[slot["state"], bi, rows, :]
                z = za_ref[bi, rows, :]
                y_ref[bi, rows, :] = (yn * (z * _sigmoid(z))).astype(y_ref.dtype)
                yield

    state_ticks = 1 + 3 * n_chunks + (2 if finalize else 0)

    def step_body(phase):
        if isinstance(phase, int):
            back = lambda k: (phase + RING - k) % RING
        else:
            back = lambda k: lax.rem(phase + (RING - k), RING)
        slot.update(prep=phase, matrix=back(1), state=back(2))
        _interleave([(prep_stream(), prep_ticks), (matrix_stream(), matrix_ticks)]
                    + [(state_stream(bi), state_ticks) for bi in range(nbat)])

    if static_ring:
        for phase in range(RING):
            pl.when(lax.rem(i, RING) == phase)(functools.partial(step_body, phase))
    else:
        step_body(lax.rem(i, RING))

    if emit_state:
        @pl.when(last2)
        def _():
            sout_ref[...] = s_scr[...]


def _scan(rkv, lora, s0c, consts, *, reverse, finalize, emit_state, nbat, tb, yf=None, za=None):
    nb, L, _ = rkv.shape
    nj = L // tb
    n_items = (nb // nbat) * nj
    zero_init = s0c is None
    d = 1 if reverse else 0

    def pos(item):
        jpos = lax.rem(item, nj)
        return item // nj, ((nj - 1 - jpos) if reverse else jpos)

    def item0(i):
        return jnp.minimum(i, n_items - 1)

    def item2(i):
        return jnp.clip(i - 2, 0, n_items - 1)

    def tok(w, item_of):
        def imap(i):
            p, jj = pos(item_of(i))
            return (p, jj, 0)
        return pl.BlockSpec((nbat, tb, w), imap)

    full2 = lambda a: pl.BlockSpec(a.shape, lambda i: (0, 0))
    st_spec = pl.BlockSpec((nbat, HEAD, D_A), lambda i: (pos(item2(i))[0], 0, 0))

    args = [rkv, lora]
    in_specs = [tok(D_RKV, item0), tok(D_LORA, item0)]
    if not zero_init:
        args.append(s0c)
        in_specs.append(st_spec)
    small = [consts["pv"][d], consts["wup"][d], consts["aup"][d], consts["tri"][d], consts["ones_bd"]]
    args += small
    in_specs += [full2(a) for a in small]
    if finalize:
        args += [yf, za, consts["pv2"], consts["aup"][0]]
        in_specs += [tok(D_A, item2), tok(D_A, item2), full2(consts["pv2"]), full2(consts["aup"][0])]

    out_shape = [jax.ShapeDtypeStruct((nb, L, D_A), BF16 if finalize else F32)]
    out_specs = [tok(D_A, item2)]
    if emit_state:
        out_shape.append(jax.ShapeDtypeStruct((nb, HEAD, D_A), F32))
        out_specs.append(st_spec)

    tokbuf = lambda dt: pltpu.VMEM((RING, nbat, tb, D_A), dt)
    scratch = [pltpu.VMEM((nbat, HEAD, D_A), F32)]
    scratch += [tokbuf(BF16)] * 3
    scratch += [pltpu.VMEM((RING, nbat, tb // SCAN_CHUNK, HEAD, D_A), F32)]
    scratch += [pltpu.VMEM((RING, nbat, tb // SCAN_CHUNK, D_A, 2 * SCAN_CHUNK), BF16)] * 2
    if finalize:
        scratch += [tokbuf(F32)]
    scratch += [tokbuf(BF16)] * 2 + [tokbuf(F32)] * 2
    if finalize:
        scratch += [pltpu.VMEM((nbat, tb, D_A), F32)]

    static_ring = (not finalize) or n_items >= STATIC_RING_MIN_ITEMS
    kern = functools.partial(_scan_kernel, reverse=reverse, finalize=finalize, zero_init=zero_init,
                             emit_state=emit_state, nbat=nbat, tb=tb, nj=nj, n_items=n_items,
                             static_ring=static_ring)
    outs = pl.pallas_call(
        kern,
        grid=(n_items + 2,),
        in_specs=in_specs,
        out_specs=out_specs,
        out_shape=out_shape,
        scratch_shapes=scratch,
        compiler_params=pltpu.CompilerParams(
            dimension_semantics=("arbitrary",), vmem_limit_bytes=VMEM_LIMIT),
        name="scan_bwd" if reverse else "scan_fwd",
    )(*args)
    return outs if emit_state else (outs[0], None)


def _out_kernel(x_ref, uvz_ref, ya_ref, mod_ref, lnpost_ref, sgu_ref, ws_ref, bs_ref, wout_ref, o_ref, *, tm):
    gate = mod_ref[0, 2:3, :]
    ln_g = sgu_ref[0:1, :]
    ln_b = sgu_ref[1:2, :]
    lane = lax.broadcasted_iota(jnp.int32, (GMLP_CHUNK, LANES), 1)
    low = lane < (D_B // H_B)
    post_gain = lnpost_ref[...] * gate

    def chunk_stream(q):
        rows = slice(q * GMLP_CHUNK, (q + 1) * GMLP_CHUNK)
        vb = uvz_ref[0, rows, D_B:2 * D_B].astype(F32)
        mu = jnp.mean(vb, axis=-1, keepdims=True)
        vc = vb - mu
        var = jnp.mean(vc * vc, axis=-1, keepdims=True)
        vn = (vc * lax.rsqrt(var + NORM_EPS)) * ln_g + ln_b
        yield
        parts = []
        for pr in range(H_B // 2):
            vp = vn[:, pr * LANES:(pr + 1) * LANES]
            rhs = jnp.concatenate([jnp.where(low, vp, 0.0), jnp.where(low, 0.0, vp)], axis=0)
            parts.append(_dot(ws_ref[pr], rhs.astype(BF16)))
        yield
        u = uvz_ref[0, rows, 0:D_B].astype(F32)
        zb = uvz_ref[0, rows, 2 * D_B:3 * D_B].astype(F32)
        s = jnp.concatenate(parts, axis=1) + bs_ref[...]
        yb = u * s * (zb * _sigmoid(zb))
        mixed = jnp.concatenate([ya_ref[0, rows, :], yb.astype(BF16)], axis=1)
        out = _dot(mixed, wout_ref[...])
        yield
        ms = jnp.mean(out * out, axis=-1, keepdims=True)
        o_ref[0, rows, :] = x_ref[0, rows, :] + (out * lax.rsqrt(ms + NORM_EPS)) * post_gain
        yield

    n_q = tm // GMLP_CHUNK
    streams = [chunk_stream(q) for q in range(n_q)]
    for step in range(n_q + 3):
        for q in range(n_q):
            if 0 <= step - q < 4:
                next(streams[q])


def _out_stage(x, uvz, ya, mod, ln_post, sgu, ws_cat, bs_x, w_out_bf, tm):
    nb, L, _ = x.shape
    tok = lambda d: pl.BlockSpec((1, tm, d), lambda b, i: (b, i, 0))
    full = lambda a: pl.BlockSpec(a.shape, lambda b, i: (0,) * a.ndim)
    return pl.pallas_call(
        functools.partial(_out_kernel, tm=tm),
        grid=(nb, L // tm),
        in_specs=[tok(D_MODEL), tok(D_UVZ), tok(D_A),
                  pl.BlockSpec((1, 3, D_MODEL), lambda b, i: (b, 0, 0)),
                  full(ln_post), full(sgu), full(ws_cat), full(bs_x), full(w_out_bf)],
        out_specs=tok(D_MODEL),
        out_shape=jax.ShapeDtypeStruct((nb, L, D_MODEL), F32),
        compiler_params=pltpu.CompilerParams(
            dimension_semantics=("arbitrary", "arbitrary"), vmem_limit_bytes=VMEM_LIMIT),
        name="out_stage",
    )(x, uvz, ya, mod, ln_post, sgu, ws_cat, bs_x, w_out_bf)


def _layer(x, mod, s0f, s0b, lw, consts, *, n_seq, seq_len, emit_state):
    nb, ntok, _ = x.shape
    per_seq = lambda a: a.reshape(n_seq, seq_len, a.shape[-1])
    rkv, za, lora, uvz = _inproj(x, mod, lw["ln_pre"], lw["ts_mu"], lw["w_in_bf"], TOKEN_TILE, seq_len)
    rkv, za, lora = per_seq(rkv), per_seq(za), per_seq(lora)
    yf, sf = _scan(rkv, lora, s0f, consts, reverse=False, finalize=False,
                   emit_state=emit_state, nbat=SCAN_BATCH, tb=SCAN_BLOCK)
    ya, sb = _scan(rkv, lora, s0b, consts, reverse=True, finalize=True,
                   emit_state=emit_state, nbat=SCAN_BATCH, tb=SCAN_BLOCK, yf=yf, za=za)
    y = _out_stage(x, uvz, ya.reshape(nb, ntok, D_A), mod, lw["ln_post"], lw["sgu"], lw["ws_cat"], lw["bs_x"],
                   lw["w_out_bf"], OUT_TILE)
    return y, sf, sb


def _pad_rows(a, n):
    return jnp.concatenate([a, jnp.zeros((n - a.shape[0],) + a.shape[1:], a.dtype)], axis=0)


def _to_compact(s):
    b = s.shape[0]
    return jnp.transpose(s, (0, 3, 1, 2)).reshape(b, HEAD, D_A)


def _from_compact(sc):
    b = sc.shape[0]
    return jnp.transpose(sc.reshape(b, HEAD, H_A, HEAD), (0, 2, 3, 1))


def kernel(x_prompt, x_sample, c, state_fwd, state_bwd, c_ctx, ln_pre, ln_post, w_mod, b_mod, w_in, ts_mu, w0, w_up, a0, a_up, k_k, k_a, r_k, gn_w, gn_b, sgu_ln_g, sgu_ln_b, w_s, b_s, w_out):
    depth = w_in.shape[0]
    batch, seq, _ = x_prompt.shape
    dec_batch, dec_seq, _ = x_sample.shape

    ti = jnp.arange(SCAN_CHUNK)
    tri = [(ti[None, :] <= ti[:, None]).astype(BF16),
           (ti[None, :] >= ti[:, None]).astype(BF16)]
    gi = jnp.arange(GROUP_LANES) // HEAD
    ones_bd = (gi[:, None] == gi[None, :]).astype(BF16)

    cvecs = _pad_rows(jnp.concatenate([c_ctx[None, :], c], axis=0), 16)

    y_ctx = x_prompt.reshape(1, batch * seq, D_MODEL)
    y_lat = x_sample
    new_f, new_b = [], []
    for l in range(depth):
        mods = _modulation(cvecs, w_mod[l], b_mod[l][None, :]).reshape(16, 3, D_MODEL)
        mod_ctx = mods[0:1]
        mod_lat = mods[1:1 + dec_batch]

        zpad = jnp.zeros((LORA, D_A), F32)
        wup = [jnp.concatenate([w_up[l, 0], zpad], 0).astype(BF16),
               jnp.concatenate([zpad, w_up[l, 1]], 0).astype(BF16)]
        aup = [jnp.concatenate([a_up[l, 0], zpad], 0).astype(BF16),
               jnp.concatenate([zpad, a_up[l, 1]], 0).astype(BF16)]
        pv = [_pad_rows(jnp.stack([w0[l, d], a0[l, d], k_k[l, d], k_a[l, d]]), 8) for d in range(2)]
        pv2 = _pad_rows(jnp.stack([a0[l, 0], k_a[l, 0], r_k[l, 0].reshape(D_A), r_k[l, 1].reshape(D_A),
                                   gn_w[l], gn_b[l]]), 8)
        consts = {"pv": pv, "pv2": pv2, "wup": wup, "aup": aup, "tri": tri, "ones_bd": ones_bd}
        lw = {
            "ln_pre": ln_pre[l][None, :], "ln_post": ln_post[l][None, :], "ts_mu": ts_mu[l],
            "w_in_bf": w_in[l].astype(BF16), "w_out_bf": w_out[l].astype(BF16),
            "sgu": jnp.stack([sgu_ln_g[l], sgu_ln_b[l]]),
            "ws_cat": jnp.concatenate([w_s[l, 0::2], w_s[l, 1::2]], axis=2).astype(BF16),
            "bs_x": jnp.repeat(b_s[l].T, D_B // H_B, axis=1),
        }

        y_ctx, sf, sb = _layer(y_ctx, mod_ctx, None, None, lw, consts,
                               n_seq=batch, seq_len=seq, emit_state=True)
        new_f.append(_from_compact(sf))
        new_b.append(_from_compact(sb))

        y_lat, _, _ = _layer(y_lat, mod_lat, _to_compact(state_fwd[:, l]), _to_compact(state_bwd[:, l]),
                             lw, consts, n_seq=dec_batch, seq_len=dec_seq, emit_state=False)

    y_prompt = y_ctx.reshape(batch, seq, D_MODEL)
    new_state_fwd = jnp.stack(new_f, axis=1).astype(x_prompt.dtype)
    new_state_bwd = jnp.stack(new_b, axis=1).astype(x_prompt.dtype)
    return (y_prompt, y_lat, new_state_fwd, new_state_bwd)
```
